```python
import math
import jax, jax.numpy as jnp
from jax import lax
import numpy as np

D_MODEL = 1024
BATCH = 4
SEQ = 4096
DEPTH = 4
DEC_BATCH = 32
DEC_SEQ = 4
PAST_LEN = 8192
PAGE_SIZE = 128

DN_ALPHA = (2 * DEPTH) ** 0.25
DN_BETA = (8 * DEPTH) ** -0.25
LN_EPS = 1e-5
D_FF = 2816
FFN_RES = 0.5
GLA_HEADS = 4
GLA_DK = 64
GLA_DV = 128
GLA_LR = 16
GLA_GATE_NORM = 16.0
GLA_CHUNK = 64
SB_HEADS = 8
SB_DH = 64
SB_BLOCK = 128
SB_BIAS_INIT = -7.0
RW_HEADS = 8
RW_N = 64
RW_W = RW_HEADS * RW_N
RW_LR_W = 64
RW_LR_A = 64
RW_LR_G = 128
RW_LN_EPS = 64e-5
N_BRANCH = 3
BRANCH_W = 512
MEM_LEN = 256
MEM_HEADS = 4
MEM_DH = D_MODEL // MEM_HEADS
MEM_W = MEM_HEADS * MEM_DH
GLA_QK = GLA_HEADS * GLA_DK
GLA_V = GLA_HEADS * GLA_DV
SB_W = SB_HEADS * SB_DH
GLA_COLS = 2 * GLA_QK + 2 * GLA_V + GLA_LR
SB_COLS = 3 * SB_W
RW_COLS = 3 * RW_W + RW_LR_W + RW_LR_A + RW_LR_G
GATE_COLS = N_BRANCH * D_MODEL
P_TOTAL = GLA_COLS + SB_COLS + RW_COLS + GATE_COLS

kernel_name = "hybrid_gla_stickbreak_rwkv7_decoder_step"

F32 = jnp.float32


def _split(a, sizes):
    return jnp.split(a, [int(i) for i in np.cumsum(sizes)[:-1]], axis=-1)


def _layer_norm(x, g, b):
    xf = x.astype(F32)
    mu = jnp.mean(xf, -1, keepdims=True)
    var = jnp.mean(jnp.square(xf - mu), -1, keepdims=True)
    return ((xf - mu) * lax.rsqrt(var + LN_EPS) * g + b).astype(x.dtype)


def _swiglu(x, w1, w3, w2):
    return (jax.nn.silu(x @ w1) * (x @ w3)) @ w2


def _gla(q, k, v, log_a, s0):
    bsz, T, H, _ = q.shape
    dv = v.shape[-1]
    c = math.gcd(T, GLA_CHUNK)
    n = T // c

    def blocks(t):
        return t.astype(F32).reshape(bsz, n, c, H, t.shape[-1]).transpose(1, 0, 3, 2, 4)

    qc, kc, vc = blocks(q), blocks(k), blocks(v)
    bc = jnp.cumsum(blocks(log_a), axis=3)
    causal = jnp.tril(jnp.ones((c, c), bool))

    def step(S, xs):
        qi, ki, vi, bi = xs
        b_last = bi[:, :, -1:, :]
        q_dec = qi * jnp.exp(bi)
        k_inv = ki * jnp.exp(-bi)
        att = jnp.where(causal, jnp.einsum("bhcd,bhsd->bhcs", q_dec, k_inv), 0.0)
        o = jnp.einsum("bhcd,bhde->bhce", q_dec, S) + jnp.einsum("bhcs,bhse->bhce", att, vi)
        k_end = ki * jnp.exp(b_last - bi)
        S = S * jnp.exp(b_last)[:, :, 0, :, None] + jnp.einsum("bhsd,bhse->bhde", k_end, vi)
        return S, o

    S, o = lax.scan(step, s0.astype(F32), (qc, kc, vc, bc))
    return o.transpose(1, 0, 3, 2, 4).reshape(bsz, T, H, dv), S


def _stick_breaking(q, k, v, bias):
    bsz, tq, h, dh = q.shape
    tk = k.shape[1]
    offset = tk - tq
    outs = []
    for start in range(0, tq, SB_BLOCK):
        stop = min(start + SB_BLOCK, tq)
        kend = min(tk, offset + stop - 1)
        if kend <= 0:
            outs.append(jnp.zeros((bsz, stop - start, h, dh), v.dtype))
            continue
        z = jnp.einsum("bqhd,bkhd->bhqk", q[:, start:stop], k[:, :kend]).astype(F32) * (dh ** -0.5)
        z = z + bias.astype(F32)[None, :, None, None]
        mask = jnp.arange(kend)[None, :] < (offset + jnp.arange(start, stop))[:, None]
        log_1m = jnp.where(mask, jax.nn.log_sigmoid(-z), 0.0)
        between = lax.cumsum(log_1m, axis=3, reverse=True) - log_1m
        w = jnp.where(mask, jnp.exp(jax.nn.log_sigmoid(z) + between), 0.0)
        outs.append(jnp.einsum("bhqk,bkhd->bqhd", w.astype(v.dtype), v[:, :kend]))
    return jnp.concatenate(outs, axis=1)


def _rwkv7(r, decay, k, v, a, b, s0):
    def tm(t):
        return jnp.moveaxis(t, 1, 0)

    def step(S, xs):
        r_t, d_t, k_t, v_t, a_t, b_t = xs
        sa = jnp.einsum("bhvk,bhk->bhv", S, a_t)
        S = S * d_t[:, :, None, :] + sa[..., None] * b_t[:, :, None, :] + v_t[..., None] * k_t[:, :, None, :]
        return S, jnp.einsum("bhvk,bhk->bhv", S, r_t)

    S, y = lax.scan(step, s0.astype(F32), (tm(r), tm(decay), tm(k), tm(v), tm(a), tm(b)))
    return jnp.moveaxis(y, 0, 1), S


def _head_ln(y, g, b):
    mu = jnp.mean(y, -1, keepdims=True)
    var = jnp.mean(jnp.square(y - mu), -1, keepdims=True)
    return (y - mu) * lax.rsqrt(var + RW_LN_EPS) * g.reshape(RW_HEADS, RW_N) + b.reshape(RW_HEADS, RW_N)


def _layer(x, p, mem_k, mem_v, sb_past_k, sb_past_v, gla_s0, rw_s0, rw_shift0):
    bsz, T, _ = x.shape

    def heads(t, h):
        return t.reshape(bsz, T, h, -1)

    x = _layer_norm(DN_ALPHA * x + FFN_RES * _swiglu(x, p["ffn_w1"][0], p["ffn_w3"][0], p["ffn_w2"][0]),
                    p["ln_g"][0], p["ln_b"][0])
    pa, pb, pc, pg = _split(x @ p["w_in"], (GLA_COLS, SB_COLS, RW_COLS, GATE_COLS))
    qa, ka, va, ga, gda = _split(pa, (GLA_QK, GLA_QK, GLA_V, GLA_V, GLA_LR))
    log_a = jax.nn.log_sigmoid((gda @ p["gla_wg2"] + p["gla_bg"]).astype(F32)) / GLA_GATE_NORM
    oa, gla_s = _gla(heads(qa, GLA_HEADS) * (GLA_DK ** -0.5), heads(ka, GLA_HEADS), heads(va, GLA_HEADS),
                     heads(log_a, GLA_HEADS), gla_s0)
    oa = oa * lax.rsqrt(jnp.mean(oa * oa, -1, keepdims=True) + LN_EPS) * p["gla_norm_g"]
    oa = (oa.reshape(bsz, T, GLA_V) * jax.nn.silu(ga.astype(F32))).astype(x.dtype)
    qb, kb, vb = [heads(t, SB_HEADS) for t in _split(pb, (SB_W, SB_W, SB_W))]
    if sb_past_k is None:
        k_all, v_all = kb, vb
    else:
        k_all = jnp.concatenate([sb_past_k.astype(kb.dtype), kb], axis=1)
        v_all = jnp.concatenate([sb_past_v.astype(vb.dtype), vb], axis=1)
    ob = _stick_breaking(qb, k_all, v_all, p["sb_bias"]).reshape(bsz, T, SB_W).astype(x.dtype)
    prev = jnp.concatenate([rw_shift0[:, None, :].astype(pc.dtype), pc[:, :-1]], axis=1)
    xc = pc + (prev - pc) * p["rw_mu"]
    r, kc, vc, wd, ad, gd = _split(xc, (RW_W, RW_W, RW_W, RW_LR_W, RW_LR_A, RW_LR_G))
    w_log = -jax.nn.softplus(-(p["rw_w0"] + jnp.tanh(wd) @ p["rw_w2"]).astype(F32)) - 0.5
    decay = jnp.exp(-jnp.exp(w_log))
    a = jax.nn.sigmoid((p["rw_a0"] + ad @ p["rw_a2"]).astype(F32))
    g = (jax.nn.sigmoid(gd) @ p["rw_g2"]).astype(F32)
    kk = heads((kc * p["rw_kk"]).astype(F32), RW_HEADS)
    kk = kk / jnp.maximum(jnp.sqrt(jnp.sum(kk * kk, -1, keepdims=True)), 1e-12)
    kmod = kc.astype(F32) * (1.0 + (a - 1.0) * p["rw_ka"])
    rh = heads(r.astype(F32), RW_HEADS)
    kh = heads(kmod, RW_HEADS)
    vh = heads(vc.astype(F32), RW_HEADS)
    ah = heads(a, RW_HEADS)
    y, rw_s = _rwkv7(rh, heads(decay, RW_HEADS), kh, vh, -kk, kk * ah, rw_s0)
    y = _head_ln(y, p["rw_lnx_g"], p["rw_lnx_b"])
    y = y + jnp.sum(rh * kh * p["rw_rk"], -1, keepdims=True) * vh
    oc = (y.reshape(bsz, T, RW_W) * g).astype(x.dtype)
    new_shift = pc[:, -1]
    gates = jax.nn.sigmoid(pg.reshape(bsz, T, N_BRANCH, D_MODEL) + p["b_gate"])
    br = jnp.einsum("btnc,ncd->btnd", jnp.stack([oa, ob, oc], axis=2), p["w_br"])
    mix = jnp.sum(gates * br, axis=2) @ p["w_o"]
    x = _layer_norm(DN_ALPHA * x + mix, p["ln_g"][1], p["ln_b"][1])
    qm = heads(x @ p["mem_wq"], MEM_HEADS)
    s = jnp.einsum("bthd,bmhd->bhtm", qm, mem_k).astype(F32) * (MEM_DH ** -0.5)
    att = jax.nn.softmax(s, axis=-1).astype(x.dtype)
    xm = jnp.einsum("bhtm,bmhd->bthd", att, mem_v.astype(x.dtype)).reshape(bsz, T, MEM_W) @ p["mem_wo"]
    x = _layer_norm(DN_ALPHA * x + xm, p["ln_g"][2], p["ln_b"][2])
    x = _layer_norm(DN_ALPHA * x + FFN_RES * _swiglu(x, p["ffn_w1"][1], p["ffn_w3"][1], p["ffn_w2"][1]),
                    p["ln_g"][3], p["ln_b"][3])
    return x, kb, vb, gla_s.astype(x.dtype), rw_s.astype(x.dtype), new_shift


def setup_inputs(seed: int = 0) -> dict:
    key = jax.random.key(seed)
    keys = jax.random.split(key, 48)
    counter = [0]

    def nxt():
        k = keys[counter[0]]
        counter[0] += 1
        return k

    def nrm(shape, scale=1.0):
        return jax.random.normal(nxt(), shape, F32) * scale

    n_pages = PAST_LEN // PAGE_SIZE
    n_pool = (DEC_BATCH * n_pages * 5) // 4
    perm = jax.random.permutation(nxt(), n_pool)[: DEC_BATCH * n_pages]
    page_table = perm.reshape(DEC_BATCH, n_pages).astype(jnp.int32)
    return {
        "x_prompt": nrm((BATCH, SEQ, D_MODEL)),
        "x_sample": nrm((DEC_BATCH, DEC_SEQ, D_MODEL)),
        "mem_prompt": nrm((BATCH, MEM_LEN, D_MODEL)),
        "cache_sb_k": nrm((DEPTH, n_pool, PAGE_SIZE, SB_HEADS, SB_DH)),
        "cache_sb_v": nrm((DEPTH, n_pool, PAGE_SIZE, SB_HEADS, SB_DH)),
        "page_table": page_table,
        "state_gla": nrm((DEPTH, DEC_BATCH, GLA_HEADS, GLA_DK, GLA_DV), 0.1),
        "state_rwkv": nrm((DEPTH, DEC_BATCH, RW_HEADS, RW_N, RW_N), 0.1),
        "state_rwkv_shift": nrm((DEPTH, DEC_BATCH, RW_COLS)),
        "cache_mem_k": nrm((DEPTH, DEC_BATCH, MEM_LEN, MEM_HEADS, MEM_DH)),
        "cache_mem_v": nrm((DEPTH, DEC_BATCH, MEM_LEN, MEM_HEADS, MEM_DH)),
        "ln_g": 1.0 + nrm((DEPTH, 4, D_MODEL), 0.02),
        "ln_b": nrm((DEPTH, 4, D_MODEL), 0.02),
        "ffn_w1": nrm((DEPTH, 2, D_MODEL, D_FF), D_MODEL ** -0.5),
        "ffn_w3": nrm((DEPTH, 2, D_MODEL, D_FF), D_MODEL ** -0.5),
        "ffn_w2": nrm((DEPTH, 2, D_FF, D_MODEL), DN_BETA * D_FF ** -0.5),
        "w_in": nrm((DEPTH, D_MODEL, P_TOTAL), D_MODEL ** -0.5),
        "b_gate": nrm((DEPTH, N_BRANCH, D_MODEL), 0.02),
        "gla_wg2": nrm((DEPTH, GLA_LR, GLA_QK), GLA_LR ** -0.5),
        "gla_bg": nrm((DEPTH, GLA_QK), 0.1),
        "gla_norm_g": 1.0 + nrm((DEPTH, GLA_DV), 0.02),
        "sb_bias": SB_BIAS_INIT + nrm((DEPTH, SB_HEADS), 0.1),
        "rw_mu": jax.random.uniform(nxt(), (DEPTH, RW_COLS), F32),
        "rw_w0": nrm((DEPTH, RW_W), 0.5) - 1.0,
        "rw_w2": nrm((DEPTH, RW_LR_W, RW_W), 0.1 * RW_LR_W ** -0.5),
        "rw_a0": nrm((DEPTH, RW_W), 0.1),
        "rw_a2": nrm((DEPTH, RW_LR_A, RW_W), 0.1 * RW_LR_A ** -0.5),
        "rw_g2": nrm((DEPTH, RW_LR_G, RW_W), RW_LR_G ** -0.5),
        "rw_kk": 0.85 + nrm((DEPTH, RW_W), 0.02),
        "rw_ka": 1.0 + nrm((DEPTH, RW_W), 0.02),
        "rw_rk": nrm((DEPTH, RW_HEADS, RW_N), 0.1),
        "rw_lnx_g": 1.0 + nrm((DEPTH, RW_W), 0.02),
        "rw_lnx_b": nrm((DEPTH, RW_W), 0.02),
        "w_br": nrm((DEPTH, N_BRANCH, BRANCH_W, D_MODEL), DN_BETA * BRANCH_W ** -0.5),
        "w_o": nrm((DEPTH, D_MODEL, D_MODEL), DN_BETA * D_MODEL ** -0.5),
        "mem_wq": nrm((DEPTH, D_MODEL, MEM_W), D_MODEL ** -0.5),
        "mem_wk": nrm((DEPTH, D_MODEL, MEM_W), D_MODEL ** -0.5),
        "mem_wv": nrm((DEPTH, D_MODEL, MEM_W), D_MODEL ** -0.5),
        "mem_wo": nrm((DEPTH, MEM_W, D_MODEL), DN_BETA * MEM_W ** -0.5),
    }


def reference(x_prompt, x_sample, mem_prompt, cache_sb_k, cache_sb_v, page_table, state_gla, state_rwkv,
              state_rwkv_shift, cache_mem_k, cache_mem_v, ln_g, ln_b, ffn_w1, ffn_w3, ffn_w2, w_in, b_gate,
              gla_wg2, gla_bg, gla_norm_g, sb_bias, rw_mu, rw_w0, rw_w2, rw_a0, rw_a2, rw_g2, rw_kk, rw_ka,
              rw_rk, rw_lnx_g, rw_lnx_b, w_br, w_o, mem_wq, mem_wk, mem_wv, mem_wo):
    bp = x_prompt.shape[0]
    db = x_sample.shape[0]
    n_pages = page_table.shape[1]
    yp, ys = x_prompt, x_sample
    kp_l, vp_l, gp_l, rp_l, shp_l, mk_l, mv_l = [], [], [], [], [], [], []
    ks_l, vs_l, gs_l, rs_l, shs_l = [], [], [], [], []
    for l in range(DEPTH):
        p = dict(ln_g=ln_g[l], ln_b=ln_b[l], ffn_w1=ffn_w1[l], ffn_w3=ffn_w3[l], ffn_w2=ffn_w2[l],
                 w_in=w_in[l], b_gate=b_gate[l], gla_wg2=gla_wg2[l], gla_bg=gla_bg[l],
                 gla_norm_g=gla_norm_g[l], sb_bias=sb_bias[l], rw_mu=rw_mu[l], rw_w0=rw_w0[l],
                 rw_w2=rw_w2[l], rw_a0=rw_a0[l], rw_a2=rw_a2[l], rw_g2=rw_g2[l], rw_kk=rw_kk[l],
                 rw_ka=rw_ka[l], rw_rk=rw_rk[l], rw_lnx_g=rw_lnx_g[l], rw_lnx_b=rw_lnx_b[l],
                 w_br=w_br[l], w_o=w_o[l], mem_wq=mem_wq[l], mem_wo=mem_wo[l])
        mk = (mem_prompt @ mem_wk[l]).reshape(bp, MEM_LEN, MEM_HEADS, MEM_DH)
        mv = (mem_prompt @ mem_wv[l]).reshape(bp, MEM_LEN, MEM_HEADS, MEM_DH)
        gla0 = jnp.zeros((bp, GLA_HEADS, GLA_DK, GLA_DV), yp.dtype)
        rw0 = jnp.zeros((bp, RW_HEADS, RW_N, RW_N), yp.dtype)
        sh0 = jnp.zeros((bp, RW_COLS), yp.dtype)
        yp, kp, vp, gp, rp, shp = _layer(yp, p, mk, mv, None, None, gla0, rw0, sh0)
        kp_l.append(kp); vp_l.append(vp); gp_l.append(gp); rp_l.append(rp); shp_l.append(shp)
        mk_l.append(mk); mv_l.append(mv)
        past_k = cache_sb_k[l][page_table].reshape(db, n_pages * PAGE_SIZE, SB_HEADS, SB_DH)
        past_v = cache_sb_v[l][page_table].reshape(db, n_pages * PAGE_SIZE, SB_HEADS, SB_DH)
        ys, ks, vs, gs, rs, shs = _layer(ys, p, cache_mem_k[l], cache_mem_v[l], past_k, past_v,
                                         state_gla[l], state_rwkv[l], state_rwkv_shift[l])
        ks_l.append(ks); vs_l.append(vs); gs_l.append(gs); rs_l.append(rs); shs_l.append(shs)
    return (yp, ys,
            jnp.stack(kp_l), jnp.stack(vp_l), jnp.stack(gp_l), jnp.stack(rp_l), jnp.stack(shp_l),
            jnp.stack(mk_l), jnp.stack(mv_l),
            jnp.stack(ks_l), jnp.stack(vs_l), jnp.stack(gs_l), jnp.stack(rs_l), jnp.stack(shs_l))
```

```python
import functools
import math

import jax
import jax.numpy as jnp
from jax import lax
from jax.experimental import pallas as pl
from jax.experimental.pallas import tpu as pltpu

F32 = jnp.float32
BF16 = jnp.bfloat16

DEPTH = 4
DN_ALPHA = (2 * DEPTH) ** 0.25
LN_EPS = 1e-5
FFN_RES = 0.5
GLA_HEADS, GLA_DK, GLA_DV, GLA_LR = 4, 64, 128, 16
GLA_GATE_NORM = 16.0
SB_HEADS, SB_DH = 8, 64
RW_HEADS, RW_N = 8, 64
RW_LR_W, RW_LR_A, RW_LR_G = 64, 64, 128
RW_LN_EPS = 64e-5
N_BRANCH = 3
MEM_HEADS = 4
PAGE_SIZE = 128

GLA_QK = GLA_HEADS * GLA_DK
GLA_V = GLA_HEADS * GLA_DV
SB_W = SB_HEADS * SB_DH
RW_W = RW_HEADS * RW_N
RW_LR = RW_LR_W + RW_LR_A + RW_LR_G
RW_COLS = 3 * RW_W + RW_LR

COL_GLA_Q, COL_GLA_K, COL_GLA_V, COL_GLA_G = 0, 256, 512, 1024
COL_SB_Q, COL_SB_K, COL_SB_V = 1536, 2048, 2560
COL_GATE = 3072
COL_RW = 6144
COL_RW_LR = COL_RW + 3 * RW_W
COL_GDA = COL_RW + RW_COLS
P_COLS = 8192

LANES = 128
CHUNK = 64
VMEM_LIMIT = 56 * 1024 * 1024


def _cparams(sem):
    return pltpu.CompilerParams(dimension_semantics=sem, vmem_limit_bytes=VMEM_LIMIT)


def _iota(shape, dim):
    return lax.broadcasted_iota(jnp.int32, shape, dim)


_DN = {"nn": (((1,), (0,)), ((), ())), "nt": (((1,), (1,)), ((), ())), "tn": (((0,), (0,)), ((), ()))}


def _pdot(a, b, kind="nn", passes=1):
    dn = _DN[kind]

    def f(x, y):
        return lax.dot_general(x, y, dn, preferred_element_type=F32)

    ah = a.astype(BF16)
    bh = b.astype(BF16)
    if passes == 1:
        return f(ah, bh)
    al = (a - ah.astype(F32)).astype(BF16)
    bl = (b - bh.astype(F32)).astype(BF16)
    return f(ah, bh) + (f(ah, bl) + f(al, bh))


def _split3(x):
    h = x.astype(BF16)
    r = x - h.astype(F32)
    m = r.astype(BF16)
    lo = (r - m.astype(F32)).astype(BF16)
    return h, m, lo


def _exact_left(mat_bf16, x):
    h, m, lo = _split3(x)
    d = lambda y: jnp.dot(mat_bf16, y, preferred_element_type=F32)
    return d(h) + (d(m) + d(lo))


def _exact_right(x, mat_bf16, terms=3):
    h, m, lo = _split3(x)
    d = lambda y: jnp.dot(y, mat_bf16, preferred_element_type=F32)
    if terms == 2:
        return d(h) + d(m)
    return d(h) + (d(m) + d(lo))


def _layer_norm(y, g, b):
    mu = jnp.mean(y, axis=-1, keepdims=True)
    d = y - mu
    var = jnp.mean(d * d, axis=-1, keepdims=True)
    return d * lax.rsqrt(var + LN_EPS) * g + b


def _log_sigmoid(x):
    return jnp.minimum(x, 0.0) - jnp.log1p(jnp.exp(-jnp.abs(x)))


def _ffn_ln_body(x_ref, w1_ref, w3_ref, w2_ref, g_ref, b_ref, o_ref, xb_ref, acc_ref):
    j = pl.program_id(1)

    @pl.when(j == 0)
    def _():
        xb_ref[...] = x_ref[...].astype(BF16)
        acc_ref[...] = jnp.zeros_like(acc_ref)

    xb = xb_ref[...]
    h1 = jnp.dot(xb, w1_ref[...], preferred_element_type=F32)
    h3 = jnp.dot(xb, w3_ref[...], preferred_element_type=F32)
    h = (h1 * jax.nn.sigmoid(h1)) * h3
    acc_ref[...] += jnp.dot(h.astype(BF16), w2_ref[...], preferred_element_type=F32)

    @pl.when(j == pl.num_programs(1) - 1)
    def _():
        y = DN_ALPHA * x_ref[...] + FFN_RES * acc_ref[...]
        o_ref[...] = _layer_norm(y, g_ref[...], b_ref[...])


def _ffn_ln(x, w1, w3, w2, g, b):
    n, d = x.shape
    dff = w1.shape[1]
    tm = min(n, 1024)
    tf = 256
    assert n % tm == 0 and dff % tf == 0
    return pl.pallas_call(
        _ffn_ln_body,
        grid=(n // tm, dff // tf),
        in_specs=[
            pl.BlockSpec((tm, d), lambda i, j: (i, 0)),
            pl.BlockSpec((d, tf), lambda i, j: (0, j)),
            pl.BlockSpec((d, tf), lambda i, j: (0, j)),
            pl.BlockSpec((tf, d), lambda i, j: (j, 0)),
            pl.BlockSpec((1, d), lambda i, j: (0, 0)),
            pl.BlockSpec((1, d), lambda i, j: (0, 0)),
        ],
        out_specs=pl.BlockSpec((tm, d), lambda i, j: (i, 0)),
        out_shape=jax.ShapeDtypeStruct((n, d), F32),
        scratch_shapes=[pltpu.VMEM((tm, d), BF16), pltpu.VMEM((tm, d), F32)],
        compiler_params=_cparams(("parallel", "arbitrary")),
        name="ffn_ln",
    )(x, w1, w3, w2, g, b)


def _mm_body(x_ref, w_ref, o_ref, xb_ref):
    @pl.when(pl.program_id(1) == 0)
    def _():
        xb_ref[...] = x_ref[...].astype(BF16)

    o_ref[...] = jnp.dot(xb_ref[...], w_ref[...], preferred_element_type=F32)


def _matmul(x, w):
    n, k = x.shape
    m = w.shape[1]
    tm = min(n, 1024)
    tn = min(m, 1024)
    assert n % tm == 0 and m % tn == 0
    return pl.pallas_call(
        _mm_body,
        grid=(n // tm, m // tn),
        in_specs=[
            pl.BlockSpec((tm, k), lambda i, j: (i, 0)),
            pl.BlockSpec((k, tn), lambda i, j: (0, j)),
        ],
        out_specs=pl.BlockSpec((tm, tn), lambda i, j: (i, j)),
        out_shape=jax.ShapeDtypeStruct((n, m), F32),
        scratch_shapes=[pltpu.VMEM((tm, k), BF16)],
        compiler_params=_cparams(("parallel", "arbitrary")),
        name="proj_matmul",
    )(x, w)


def _merge_body(x_ref, oa_ref, ob_ref, oc_ref, pg0_ref, pg1_ref, pg2_ref, bg_ref, wbr_ref, wo_ref,
                g_ref, b_ref, o_ref):
    s = None
    for n, (o_r, pg_r) in enumerate(((oa_ref, pg0_ref), (ob_ref, pg1_ref), (oc_ref, pg2_ref))):
        br = jnp.dot(o_r[...].astype(BF16), wbr_ref[n], preferred_element_type=F32)
        term = jax.nn.sigmoid(pg_r[...] + bg_ref[n]) * br
        s = term if s is None else s + term
    mix = jnp.dot(s.astype(BF16), wo_ref[...], preferred_element_type=F32)
    o_ref[...] = _layer_norm(DN_ALPHA * x_ref[...] + mix, g_ref[...], b_ref[...])


def _merge_ln(x, oa, ob, oc, proj, b_gate, w_br, w_o, g, b):
    n, d = x.shape
    bw = oa.shape[1]
    tm = min(n, 512)
    gate_blk = COL_GATE // d
    row = lambda i: (i, 0)
    return pl.pallas_call(
        _merge_body,
        grid=(n // tm,),
        in_specs=[
            pl.BlockSpec((tm, d), row),
            pl.BlockSpec((tm, bw), row),
            pl.BlockSpec((tm, bw), row),
            pl.BlockSpec((tm, bw), row),
            pl.BlockSpec((tm, d), lambda i: (i, gate_blk)),
            pl.BlockSpec((tm, d), lambda i: (i, gate_blk + 1)),
            pl.BlockSpec((tm, d), lambda i: (i, gate_blk + 2)),
            pl.BlockSpec((N_BRANCH, 1, d), lambda i: (0, 0, 0)),
            pl.BlockSpec((N_BRANCH, bw, d), lambda i: (0, 0, 0)),
            pl.BlockSpec((d, d), lambda i: (0, 0)),
            pl.BlockSpec((1, d), lambda i: (0, 0)),
            pl.BlockSpec((1, d), lambda i: (0, 0)),
        ],
        out_specs=pl.BlockSpec((tm, d), row),
        out_shape=jax.ShapeDtypeStruct((n, d), F32),
        compiler_params=_cparams(("parallel",)),
        name="merge_ln",
    )(x, oa, ob, oc, proj, proj, proj, b_gate, w_br, w_o, g, b)


def _memattn_body(x_ref, wq_ref, wo_ref, mk_ref, mv_ref, g_ref, b_ref, o_ref, q_ref, acc_ref, *,
                  t_seq, tiles_per_batch, masked):
    i = pl.program_id(0)
    j = pl.program_id(1)
    tm, d = x_ref.shape
    dh = d // MEM_HEADS

    @pl.when(j == 0)
    def _():
        q_ref[...] = jnp.dot(x_ref[...].astype(BF16), wq_ref[...], preferred_element_type=F32).astype(BF16)
        acc_ref[...] = jnp.zeros_like(acc_ref)

    if masked:
        row = i * tm + _iota((tm, 1), 0)
        mem_b = i // tiles_per_batch + j
        keep = (row >= mem_b * t_seq) & (row < (mem_b + 1) * t_seq)
    for h in range(MEM_HEADS):
        sl = slice(h * dh, (h + 1) * dh)
        qh = q_ref[:, sl]
        kh = mk_ref[0, :, sl].astype(BF16)
        vh = mv_ref[0, :, sl].astype(BF16)
        s = lax.dot_general(qh, kh, _DN["nt"], preferred_element_type=F32) * (dh ** -0.5)
        e = jnp.exp(s - jnp.max(s, axis=-1, keepdims=True))
        att = e / jnp.sum(e, axis=-1, keepdims=True)
        oh = jnp.dot(att.astype(BF16), vh, preferred_element_type=F32)
        if masked:
            oh = jnp.where(keep, oh, 0.0)
        acc_ref[:, sl] += oh

    @pl.when(j == pl.num_programs(1) - 1)
    def _():
        xm = jnp.dot(acc_ref[...].astype(BF16), wo_ref[...], preferred_element_type=F32)
        o_ref[...] = _layer_norm(DN_ALPHA * x_ref[...] + xm, g_ref[...], b_ref[...])


def _memattn_ln(x, wq, wo, mem_k, mem_v, g, b, t_seq):
    n, d = x.shape
    nb, m, _ = mem_k.shape
    tm = min(n, 512)
    if t_seq >= tm:
        assert t_seq % tm == 0
        tiles_per_batch, nj, masked = t_seq // tm, 1, False
    else:
        assert tm % t_seq == 0 and n == tm
        tiles_per_batch, nj, masked = 1, nb, True
    body = functools.partial(_memattn_body, t_seq=t_seq, tiles_per_batch=tiles_per_batch, masked=masked)
    mem_map = lambda i, j: (i // tiles_per_batch + j, 0, 0)
    return pl.pallas_call(
        body,
        grid=(n // tm, nj),
        in_specs=[
            pl.BlockSpec((tm, d), lambda i, j: (i, 0)),
            pl.BlockSpec((d, d), lambda i, j: (0, 0)),
            pl.BlockSpec((d, d), lambda i, j: (0, 0)),
            pl.BlockSpec((1, m, d), mem_map),
            pl.BlockSpec((1, m, d), mem_map),
            pl.BlockSpec((1, d), lambda i, j: (0, 0)),
            pl.BlockSpec((1, d), lambda i, j: (0, 0)),
        ],
        out_specs=pl.BlockSpec((tm, d), lambda i, j: (i, 0)),
        out_shape=jax.ShapeDtypeStruct((n, d), F32),
        scratch_shapes=[pltpu.VMEM((tm, d), BF16), pltpu.VMEM((tm, d), F32)],
        compiler_params=_cparams(("parallel", "arbitrary")),
        name="memattn_ln",
    )(x, wq, wo, mem_k, mem_v, g, b)


def _stack2(x):
    return jnp.concatenate([x, x], axis=0)


def _head_rows_mask(c, width, seg):
    r = _iota((2 * c, width), 0)
    l = _iota((2 * c, width), 1)
    return ((r < c) & (l < seg)) | ((r >= c) & (l >= seg))


def _gla_body(q_ref, k_ref, v_ref, g_ref, gda_ref, wg2_ref, bg_ref, ng_ref, s0_ref, o_ref, sout_ref, s_ref, *,
              t_valid):
    t = pl.program_id(1)
    tt = q_ref.shape[0]
    c = CHUNK
    npair = GLA_HEADS // 2
    kw, vw = 2 * GLA_DK, 2 * GLA_DV

    @pl.when(t == 0)
    def _():
        s_ref[...] = s0_ref[0]

    ri = _iota((c, c), 0)
    ci = _iota((c, c), 1)
    tri_incl = (ci <= ri).astype(F32).astype(BF16)
    r2 = _iota((2 * c, 2 * c), 0)
    c2 = _iota((2 * c, 2 * c), 1)
    same_head = ((r2 < c) & (c2 < c)) | ((r2 >= c) & (c2 >= c))
    att_mask = same_head & ((c2 & (c - 1)) <= (r2 & (c - 1)))
    qmask = _head_rows_mask(c, kw, GLA_DK)
    omask = _head_rows_mask(c, vw, GLA_DV)
    sr = _iota((vw, kw), 0)
    sc = _iota((vw, kw), 1)
    bd_mask = ((sr < GLA_DV) & (sc < GLA_DK)) | ((sr >= GLA_DV) & (sc >= GLA_DK))

    def chunk(ic, carry):
        rows = pl.ds(pl.multiple_of(ic * c, c), c)
        la = _log_sigmoid(_pdot(gda_ref[rows, :], wg2_ref[...]) + bg_ref[...]) / GLA_GATE_NORM
        kk = k_ref[rows, :]
        if t_valid is not None:
            valid = (t * tt + ic * c + _iota((c, 1), 0)) < t_valid
            la = jnp.where(valid, la, 0.0)
            kk = jnp.where(valid, kk, 0.0)
        bc = _exact_left(tri_incl, la)
        b_last = bc[c - 1:c, :]
        q_dec = q_ref[rows, :] * (GLA_DK ** -0.5) * jnp.exp(bc)
        k_inv = kk * jnp.exp(-bc)
        k_end = kk * jnp.exp(b_last - bc)
        e_last = jnp.exp(b_last)
        vv = v_ref[rows, :]
        outs = []
        for p in range(npair):
            kl = slice(p * kw, (p + 1) * kw)
            vl = slice(p * vw, (p + 1) * vw)
            sp = s_ref[p]
            qd2 = jnp.where(qmask, _stack2(q_dec[:, kl]), 0.0)
            att = jnp.where(att_mask, _pdot(qd2, _stack2(k_inv[:, kl]), "nt"), 0.0)
            vp = vv[:, vl]
            o2 = _pdot(qd2, sp, "nt") + jnp.where(omask, _pdot(att, _stack2(vp)), 0.0)
            outs.append(o2[:c] + o2[c:])
            s_ref[p] = sp * e_last[:, kl] + jnp.where(bd_mask, _pdot(vp, k_end[:, kl], "tn"), 0.0)
        normed = []
        for p in range(npair):
            for h in range(2):
                oh = outs[p][:, h * GLA_DV:(h + 1) * GLA_DV]
                normed.append(oh * lax.rsqrt(jnp.mean(oh * oh, axis=-1, keepdims=True) + LN_EPS))
        o = jnp.concatenate(normed, axis=1) * ng_ref[...]
        gg = g_ref[rows, :]
        o_ref[rows, :] = o * (gg * jax.nn.sigmoid(gg))
        return carry

    lax.fori_loop(0, tt // c, chunk, 0)

    @pl.when(t == pl.num_programs(1) - 1)
    def _():
        sout_ref[0] = s_ref[...]


def _gla(proj, nb, t_len, wg2p, bg, ng, s0bd, t_valid):
    tt = min(t_len, 512)
    nt = t_len // tt
    npair = GLA_HEADS // 2
    kw, vw = 2 * GLA_DK, 2 * GLA_DV
    body = functools.partial(_gla_body, t_valid=t_valid)

    def col(width, off):
        blk = off // width
        return pl.BlockSpec((tt, width), lambda b, t: (b * nt + t, blk))

    const2 = lambda b, t: (0, 0)
    return pl.pallas_call(
        body,
        grid=(nb, nt),
        in_specs=[
            col(GLA_QK, COL_GLA_Q), col(GLA_QK, COL_GLA_K), col(GLA_V, COL_GLA_V), col(GLA_V, COL_GLA_G),
            col(LANES, COL_GDA),
            pl.BlockSpec((LANES, GLA_QK), const2),
            pl.BlockSpec((1, GLA_QK), const2),
            pl.BlockSpec((1, GLA_V), const2),
            pl.BlockSpec((1, npair, vw, kw), lambda b, t: (b, 0, 0, 0)),
        ],
        out_specs=[
            pl.BlockSpec((tt, GLA_V), lambda b, t: (b * nt + t, 0)),
            pl.BlockSpec((1, npair, vw, kw), lambda b, t: (b, 0, 0, 0)),
        ],
        out_shape=[
            jax.ShapeDtypeStruct((nb * t_len, GLA_V), F32),
            jax.ShapeDtypeStruct((nb, npair, vw, kw), F32),
        ],
        scratch_shapes=[pltpu.VMEM((npair, vw, kw), F32)],
        compiler_params=_cparams(("parallel", "arbitrary")),
        name="gla",
    )(proj, proj, proj, proj, proj, wg2p, bg, ng, s0bd)


RW_PASSES = 3


def _rwkv_body(r_ref, k_ref, v_ref, lr_ref, shr_ref, shk_ref, shv_ref, shlr_ref, mur_ref, muk_ref, muv_ref,
               mulr_ref, w0_ref, w2_ref, a0_ref, a2_ref, g2_ref, kkp_ref, ka_ref, rk_ref, lng_ref, lnb_ref,
               seg_ref, s0_ref, o_ref, sout_ref,
               s_ref, cr_ref, ck_ref, cv_ref, clr_ref, rs_ref, ws_ref, ks_ref, vs_ref, as_ref, bs_ref, ys_ref, *,
               t_valid):
    t = pl.program_id(1)
    tt = r_ref.shape[0]
    c = CHUNK
    npair = RW_HEADS // 2
    pw = 2 * RW_N
    rdot = functools.partial(_pdot, passes=RW_PASSES)

    @pl.when(t == 0)
    def _():
        s_ref[...] = s0_ref[0]
        cr_ref[...] = shr_ref[0]
        ck_ref[...] = shk_ref[0]
        cv_ref[...] = shv_ref[0]
        clr_ref[...] = shlr_ref[0]

    def lerp(x_ref, carry_ref, mu_ref):
        x = x_ref[...]
        prev = pltpu.roll(x, 1, 0)
        prev = jnp.where(_iota(x.shape, 0) == 0, carry_ref[...], prev)
        carry_ref[...] = x_ref[pl.ds(tt - 1, 1), :]
        return x + (prev - x) * mu_ref[...]

    xr = lerp(r_ref, cr_ref, mur_ref)
    xk = lerp(k_ref, ck_ref, muk_ref)
    xv = lerp(v_ref, cv_ref, muv_ref)
    xlr = lerp(lr_ref, clr_ref, mulr_ref)
    w_log = _log_sigmoid(w0_ref[...] + _pdot(jnp.tanh(xlr), w2_ref[...])) - 0.5
    wdec = -jnp.exp(w_log)
    a = jax.nn.sigmoid(a0_ref[...] + _pdot(xlr, a2_ref[...]))
    g = _pdot(jax.nn.sigmoid(xlr), g2_ref[...])
    kk = xk * kkp_ref[...]
    kkn = kk / jnp.maximum(jnp.sqrt(_exact_right(kk * kk, seg_ref[...])), 1e-12)
    kmod = xk * (1.0 + (a - 1.0) * ka_ref[...])
    av = -kkn
    bv = kkn * a
    if t_valid is not None:
        valid = (t * tt + _iota((tt, 1), 0)) < t_valid
        wdec = jnp.where(valid, wdec, 0.0)
        av = jnp.where(valid, av, 0.0)
        bv = jnp.where(valid, bv, 0.0)
        kmod = jnp.where(valid, kmod, 0.0)
        xv = jnp.where(valid, xv, 0.0)
    rs_ref[...] = xr
    ws_ref[...] = wdec
    ks_ref[...] = kmod
    vs_ref[...] = xv
    as_ref[...] = av
    bs_ref[...] = bv

    ri = _iota((c, c), 0)
    ci = _iota((c, c), 1)
    tri_incl = (ci <= ri).astype(F32).astype(BF16)
    r2 = _iota((2 * c, 2 * c), 0)
    c2 = _iota((2 * c, 2 * c), 1)
    same_head = ((r2 < c) & (c2 < c)) | ((r2 >= c) & (c2 >= c))
    strict = same_head & ((c2 & (c - 1)) < (r2 & (c - 1)))
    incl = same_head & ((c2 & (c - 1)) <= (r2 & (c - 1)))
    eye = (r2 == c2).astype(F32)
    hmask = _head_rows_mask(c, pw, RW_N)
    bd_mask = _head_rows_mask(RW_N, pw, RW_N)

    def chunk(ic, carry):
        rows = pl.ds(pl.multiple_of(ic * c, c), c)
        w = ws_ref[rows, :]
        cs = _exact_left(tri_incl, w)
        c_last = cs[c - 1:c, :]
        e_neg = jnp.exp(-cs)
        e_end = jnp.exp(c_last - cs)
        e_last = jnp.exp(c_last)
        a_t = as_ref[rows, :] * jnp.exp(cs - w)
        r_t = rs_ref[rows, :] * jnp.exp(cs)
        bb = bs_ref[rows, :]
        kc = ks_ref[rows, :]
        b_t = bb * e_neg
        k_t = kc * e_neg
        b_e = bb * e_end
        k_e = kc * e_end
        vv = vs_ref[rows, :]
        ys = []
        for p in range(npair):
            ln = slice(p * pw, (p + 1) * pw)
            sp = s_ref[p]
            a2 = jnp.where(hmask, _stack2(a_t[:, ln]), 0.0)
            rr2 = jnp.where(hmask, _stack2(r_t[:, ln]), 0.0)
            b2 = _stack2(b_t[:, ln])
            k2 = _stack2(k_t[:, ln])
            v2 = _stack2(vv[:, ln])
            n_ab = jnp.where(strict, rdot(a2, b2, "nt"), 0.0)
            n_ak = jnp.where(strict, rdot(a2, k2, "nt"), 0.0)
            n_rb = jnp.where(incl, rdot(rr2, b2, "nt"), 0.0)
            n_rk = jnp.where(incl, rdot(rr2, k2, "nt"), 0.0)
            tinv = eye + n_ab
            x = n_ab
            for _ in range(int(math.log2(c)) - 1):
                x = rdot(x, x)
                tinv = tinv + rdot(tinv, x)
            w2 = rdot(a2, sp, "nt") + jnp.where(hmask, rdot(n_ak, v2), 0.0)
            u2 = rdot(tinv, w2)
            y2 = rdot(rr2, sp, "nt") + jnp.where(hmask, rdot(n_rb, u2) + rdot(n_rk, v2), 0.0)
            ys.append(y2[:c] + y2[c:])
            u = u2[:c] + u2[c:]
            upd = rdot(u, b_e[:, ln], "tn") + rdot(vv[:, ln], k_e[:, ln], "tn")
            s_ref[p] = sp * e_last[:, ln] + jnp.where(bd_mask, upd, 0.0)
        ys_ref[rows, :] = jnp.concatenate(ys, axis=1)
        return carry

    lax.fori_loop(0, tt // c, chunk, 0)

    y = ys_ref[...]
    seg = seg_ref[...]
    inv_n = 1.0 / RW_N
    mu = _exact_right(y, seg) * inv_n
    d = y - mu
    var = _exact_right(d * d, seg) * inv_n
    yn = d * lax.rsqrt(var + RW_LN_EPS) * lng_ref[...] + lnb_ref[...]
    bonus = _exact_right(xr * kmod * rk_ref[...], seg)
    o_ref[...] = (yn + bonus * xv) * g

    @pl.when(t == pl.num_programs(1) - 1)
    def _():
        sout_ref[0] = s_ref[...]


def _rwkv(proj, nb, t_len, shifts, prm, seg, s0bd, t_valid):
    tt = min(t_len, 256)
    nt = t_len // tt
    npair = RW_HEADS // 2
    pw = 2 * RW_N
    body = functools.partial(_rwkv_body, t_valid=t_valid)

    def col(width, off):
        blk = off // width
        return pl.BlockSpec((tt, width), lambda b, t: (b * nt + t, blk))

    def per_batch(width):
        return pl.BlockSpec((1, 1, width), lambda b, t: (b, 0, 0))

    def const(shape):
        return pl.BlockSpec(shape, lambda b, t: (0,) * len(shape))

    vec = const((1, RW_W))
    lrm = const((RW_LR, RW_W))
    state = pl.BlockSpec((1, npair, pw, pw), lambda b, t: (b, 0, 0, 0))
    tile = lambda: pltpu.VMEM((tt, RW_W), F32)
    return pl.pallas_call(
        body,
        grid=(nb, nt),
        in_specs=[
            col(RW_W, COL_RW), col(RW_W, COL_RW + RW_W), col(RW_W, COL_RW + 2 * RW_W), col(RW_LR, COL_RW_LR),
            per_batch(RW_W), per_batch(RW_W), per_batch(RW_W), per_batch(RW_LR),
            vec, vec, vec, const((1, RW_LR)),
            vec, lrm, vec, lrm, lrm, vec, vec, vec, vec, vec,
            const((RW_W, RW_W)),
            state,
        ],
        out_specs=[pl.BlockSpec((tt, RW_W), lambda b, t: (b * nt + t, 0)), state],
        out_shape=[
            jax.ShapeDtypeStruct((nb * t_len, RW_W), F32),
            jax.ShapeDtypeStruct((nb, npair, pw, pw), F32),
        ],
        scratch_shapes=[
            pltpu.VMEM((npair, pw, pw), F32),
            pltpu.VMEM((1, RW_W), F32), pltpu.VMEM((1, RW_W), F32), pltpu.VMEM((1, RW_W), F32),
            pltpu.VMEM((1, RW_LR), F32),
            tile(), tile(), tile(), tile(), tile(), tile(), tile(),
        ],
        compiler_params=_cparams(("parallel", "arbitrary")),
        name="rwkv7",
    )(proj, proj, proj, proj, *shifts, *prm, seg, s0bd)


def _sb_block(z, valid, r_run, mo):
    sp = jnp.maximum(z, 0.0) + jnp.log1p(jnp.exp(-jnp.abs(z)))
    lm = -sp
    if valid is not None:
        lm = jnp.where(valid, lm, 0.0)
    cum = _exact_right(lm, mo, terms=2)
    between = cum[:, :LANES] + r_run
    w = jnp.exp((z - sp) + between)
    if valid is not None:
        w = jnp.where(valid, w, 0.0)
    return w, r_run + cum[:, LANES:]


def _sbp_body(q_ref, k_ref, v_ref, bias_ref, mo_ref, o_ref):
    i = pl.program_id(2)
    tq = q_ref.shape[0]
    tk = LANES
    assert tq == tk
    hm = _head_rows_mask(tq, LANES, SB_DH)
    q2 = jnp.where(hm, _stack2(q_ref[...] * (SB_DH ** -0.5)), 0.0).astype(BF16)
    bias = bias_ref[0]
    mo = mo_ref[...]

    def block(kb, carry, masked):
        acc, r_run = carry
        rows = pl.ds(pl.multiple_of(kb * tk, tk), tk)
        kblk = k_ref[rows, :].astype(BF16)
        vblk = v_ref[rows, :].astype(BF16)
        z = lax.dot_general(q2, kblk, _DN["nt"], preferred_element_type=F32) + bias
        valid = None
        if masked:
            tpos = _iota((2 * tq, tk), 0) & (tq - 1)
            valid = _iota((2 * tq, tk), 1) < tpos
        w, r_run = _sb_block(z, valid, r_run, mo)
        acc = acc + jnp.dot(w.astype(BF16), vblk, preferred_element_type=F32)
        return acc, r_run

    zero = jnp.zeros((2 * tq, LANES), F32)
    carry = block(i, (zero, zero), True)
    acc, _ = lax.fori_loop(0, i, lambda n, cr: block(i - 1 - n, cr, False), carry)
    acc = jnp.where(hm, acc, 0.0)
    o_ref[...] = acc[:tq] + acc[tq:]


def _sb_prompt(proj, nb, t_len, bias_rows, mo):
    tq = LANES
    nq = t_len // tq
    npair = SB_HEADS // 2
    qb, kb, vb = COL_SB_Q // LANES, COL_SB_K // LANES, COL_SB_V // LANES
    return pl.pallas_call(
        _sbp_body,
        grid=(nb, npair, nq),
        in_specs=[
            pl.BlockSpec((tq, LANES), lambda b, p, i: (b * nq + i, qb + p)),
            pl.BlockSpec((t_len, LANES), lambda b, p, i: (b, kb + p)),
            pl.BlockSpec((t_len, LANES), lambda b, p, i: (b, vb + p)),
            pl.BlockSpec((1, 2 * tq, 1), lambda b, p, i: (p, 0, 0)),
            pl.BlockSpec((LANES, 2 * LANES), lambda b, p, i: (0, 0)),
        ],
        out_specs=pl.BlockSpec((tq, LANES), lambda b, p, i: (b * nq + i, p)),
        out_shape=jax.ShapeDtypeStruct((nb * t_len, SB_W), F32),
        compiler_params=_cparams(("parallel", "parallel", "arbitrary")),
        name="sb_prompt",
    )(proj, proj, proj, bias_rows, mo)


SB_PAGES_PER_STEP = 8


def _sbs_body(pt_ref, qbd_ref, bias_ref, knew_ref, vnew_ref, *rest, t_new):
    g_pages = SB_PAGES_PER_STEP
    k_refs = rest[:g_pages]
    v_refs = rest[g_pages:2 * g_pages]
    mo_ref, o_ref, acc_ref, run_ref = rest[2 * g_pages:]
    s = pl.program_id(1)
    nrow = qbd_ref.shape[1]
    q2 = (qbd_ref[0] * (SB_DH ** -0.5)).astype(BF16)
    bias = bias_ref[...]
    mo = mo_ref[...]

    def block(kblk, vblk, masked):
        z = lax.dot_general(q2, kblk.astype(BF16), _DN["nt"], preferred_element_type=F32) + bias
        valid = None
        if masked:
            key = _iota((nrow, LANES), 1)
            row = _iota((nrow, LANES), 0)
            valid = key * SB_HEADS + SB_HEADS <= row
        w, r_run = _sb_block(z, valid, run_ref[...], mo)
        run_ref[...] = r_run
        acc_ref[...] += jnp.dot(w.astype(BF16), vblk.astype(BF16), preferred_element_type=F32)

    @pl.when(s == 0)
    def _():
        acc_ref[...] = jnp.zeros_like(acc_ref)
        run_ref[...] = jnp.zeros_like(run_ref)
        block(knew_ref[0], vnew_ref[0], True)

    @pl.when(s > 0)
    def _():
        for g in range(g_pages):
            block(k_refs[g][0], v_refs[g][0], False)

    @pl.when(s == pl.num_programs(1) - 1)
    def _():
        hsel = (_iota((SB_HEADS, SB_W), 1) >= _iota((SB_HEADS, SB_W), 0) * SB_DH) & (
            _iota((SB_HEADS, SB_W), 1) < (_iota((SB_HEADS, SB_W), 0) + 1) * SB_DH)
        for tq in range(t_new):
            slab = acc_ref[tq * SB_HEADS:(tq + 1) * SB_HEADS, :]
            o_ref[0, pl.ds(tq, 1), :] = jnp.sum(jnp.where(hsel, slab, 0.0), axis=0, keepdims=True)


def _sb_sample(qbd, bias_col, knew, vnew, cache_k, cache_v, page_table, layer, mo, t_new):
    nb, nrow, w = qbd.shape
    n_pages = page_table.shape[1]
    g_pages = SB_PAGES_PER_STEP
    assert n_pages % g_pages == 0
    nsteps = n_pages // g_pages
    base = layer * (cache_k.shape[0] // DEPTH)

    def page_spec(g):
        def imap(b, s, pt):
            page = n_pages - 1 - (jnp.maximum(s, 1) - 1) * g_pages - g
            return (base + pt[b, page], 0, 0)
        return pl.BlockSpec((1, PAGE_SIZE, w), imap)

    per_b = lambda b, s, pt: (b, 0, 0)
    grid_spec = pltpu.PrefetchScalarGridSpec(
        num_scalar_prefetch=1,
        grid=(nb, nsteps + 1),
        in_specs=[
            pl.BlockSpec((1, nrow, w), per_b),
            pl.BlockSpec((nrow, 1), lambda b, s, pt: (0, 0)),
            pl.BlockSpec((1, PAGE_SIZE, w), per_b),
            pl.BlockSpec((1, PAGE_SIZE, w), per_b),
            *[page_spec(g) for g in range(g_pages)],
            *[page_spec(g) for g in range(g_pages)],
            pl.BlockSpec((LANES, 2 * LANES), lambda b, s, pt: (0, 0)),
        ],
        out_specs=pl.BlockSpec((1, t_new, w), per_b),
        scratch_shapes=[pltpu.VMEM((nrow, w), F32), pltpu.VMEM((nrow, LANES), F32)],
    )
    return pl.pallas_call(
        functools.partial(_sbs_body, t_new=t_new),
        grid_spec=grid_spec,
        out_shape=jax.ShapeDtypeStruct((nb, t_new, w), F32),
        compiler_params=_cparams(("parallel", "arbitrary")),
        name="sb_sample",
    )(page_table, qbd, bias_col, knew, vnew, *([cache_k] * g_pages), *([cache_v] * g_pages), mo)


def _blockdiag_in(s, pairs):
    b, h, do, di = s.shape
    s = s.reshape(b, pairs, 2, do, di)
    return jnp.einsum("bphvk,hg->bphvgk", s, jnp.eye(2, dtype=s.dtype)).reshape(b, pairs, 2 * do, 2 * di)


def _blockdiag_out(sbd, do, di):
    b, pairs = sbd.shape[:2]
    x = sbd.reshape(b, pairs, 2, do, 2, di)
    return jnp.einsum("bphvhk->bphvk", x).reshape(b, 2 * pairs, do, di)


def _pack_layer(l, ln_g, ln_b, ffn_w1, ffn_w3, ffn_w2, w_in, b_gate, gla_wg2, gla_bg, gla_norm_g, sb_bias, rw_mu,
                rw_w0, rw_w2, rw_a0, rw_a2, rw_g2, rw_kk, rw_ka, rw_rk, rw_lnx_g, rw_lnx_b, w_br, w_o, mem_wq,
                mem_wk, mem_wv, mem_wo):
    d = w_in.shape[1]
    gla_cols = 2 * GLA_QK + 2 * GLA_V + GLA_LR
    o_sb = gla_cols
    o_rw = o_sb + 3 * SB_W
    o_gate = o_rw + RW_COLS
    w = w_in[l]
    w_re = jnp.concatenate([
        w[:, :gla_cols - GLA_LR], w[:, o_sb:o_rw], w[:, o_gate:], w[:, o_rw:o_gate],
        w[:, gla_cols - GLA_LR:gla_cols], jnp.zeros((d, P_COLS - COL_GDA - GLA_LR), F32)], axis=1)
    assert w_re.shape[1] == P_COLS
    row = lambda v: v.reshape(1, -1)
    mu = rw_mu[l]

    def lr_pad(m, off):
        return jnp.zeros((RW_LR, RW_W), F32).at[off:off + m.shape[0]].set(m).astype(BF16)

    return dict(
        ln_g=[row(ln_g[l, i]) for i in range(4)], ln_b=[row(ln_b[l, i]) for i in range(4)],
        ffn=[(ffn_w1[l, i].astype(BF16), ffn_w3[l, i].astype(BF16), ffn_w2[l, i].astype(BF16)) for i in range(2)],
        w_in=w_re.astype(BF16),
        b_gate=b_gate[l].reshape(N_BRANCH, 1, -1), w_br=w_br[l].astype(BF16), w_o=w_o[l].astype(BF16),
        gla_wg2=jnp.zeros((LANES, GLA_QK), F32).at[:GLA_LR].set(gla_wg2[l]).astype(BF16),
        gla_bg=row(gla_bg[l]), gla_ng=row(jnp.tile(gla_norm_g[l], GLA_HEADS)),
        sb_bias=sb_bias[l],
        rw=[row(mu[:RW_W]), row(mu[RW_W:2 * RW_W]), row(mu[2 * RW_W:3 * RW_W]), row(mu[3 * RW_W:]),
            row(rw_w0[l]), lr_pad(rw_w2[l], 0), row(rw_a0[l]), lr_pad(rw_a2[l], RW_LR_W),
            lr_pad(rw_g2[l], RW_LR_W + RW_LR_A), row(rw_kk[l]), row(rw_ka[l]), row(rw_rk[l].reshape(-1)),
            row(rw_lnx_g[l]), row(rw_lnx_b[l])],
        mem_wq=mem_wq[l].astype(BF16), mem_wo=mem_wo[l].astype(BF16),
        mem_wkv=jnp.concatenate([mem_wk[l], mem_wv[l]], axis=1).astype(BF16),
    )


def _split_shift(sh):
    return [sh[:, None, :RW_W], sh[:, None, RW_W:2 * RW_W], sh[:, None, 2 * RW_W:3 * RW_W], sh[:, None, 3 * RW_W:]]


def _mixers_recurrent(proj, nb, t_len, pk, seg, gla_s0, rw_s0, rw_shift0, t_valid):
    oa, gla_s = _gla(proj, nb, t_len, pk["gla_wg2"], pk["gla_bg"], pk["gla_ng"],
                     _blockdiag_in(jnp.swapaxes(gla_s0, 2, 3), GLA_HEADS // 2), t_valid)
    oc, rw_s = _rwkv(proj, nb, t_len, _split_shift(rw_shift0), pk["rw"], seg,
                     _blockdiag_in(rw_s0, RW_HEADS // 2), t_valid)
    gla_s = jnp.swapaxes(_blockdiag_out(gla_s, GLA_DV, GLA_DK), 2, 3)
    rw_s = _blockdiag_out(rw_s, RW_N, RW_N)
    return oa, oc, gla_s, rw_s


def kernel(x_prompt, x_sample, mem_prompt, cache_sb_k, cache_sb_v, page_table, state_gla, state_rwkv,
           state_rwkv_shift, cache_mem_k, cache_mem_v, ln_g, ln_b, ffn_w1, ffn_w3, ffn_w2, w_in, b_gate,
           gla_wg2, gla_bg, gla_norm_g, sb_bias, rw_mu, rw_w0, rw_w2, rw_a0, rw_a2, rw_g2, rw_kk, rw_ka,
           rw_rk, rw_lnx_g, rw_lnx_b, w_br, w_o, mem_wq, mem_wk, mem_wv, mem_wo):
    bp, tp, d = x_prompt.shape
    db, ts, _ = x_sample.shape
    mlen = mem_prompt.shape[1]
    n_pool = cache_sb_k.shape[1]
    assert tp % CHUNK == 0 and ts <= CHUNK
    cache_k = cache_sb_k.reshape(DEPTH * n_pool, PAGE_SIZE, SB_W)
    cache_v = cache_sb_v.reshape(DEPTH * n_pool, PAGE_SIZE, SB_W)

    lane = jnp.arange(RW_W)
    seg = (lane[:, None] // RW_N == lane[None, :] // RW_N).astype(BF16)
    kidx = jnp.arange(LANES)
    mo = jnp.concatenate([(kidx[:, None] > kidx[None, :]), jnp.ones((LANES, LANES), bool)], axis=1).astype(BF16)
    head_lane = (jnp.arange(SB_W)[None, :] // SB_DH == jnp.arange(SB_HEADS)[:, None]).astype(F32)

    yp = x_prompt.reshape(bp * tp, d)
    ys = x_sample.reshape(db * ts, d)
    memp = mem_prompt.reshape(bp * mlen, d)
    outs = {k: [] for k in ("kp", "vp", "gp", "rp", "shp", "mk", "mv", "ks", "vs", "gs", "rs", "shs")}
    for l in range(DEPTH):
        pk = _pack_layer(l, ln_g, ln_b, ffn_w1, ffn_w3, ffn_w2, w_in, b_gate, gla_wg2, gla_bg, gla_norm_g, sb_bias,
                         rw_mu, rw_w0, rw_w2, rw_a0, rw_a2, rw_g2, rw_kk, rw_ka, rw_rk, rw_lnx_g, rw_lnx_b, w_br,
                         w_o, mem_wq, mem_wk, mem_wv, mem_wo)
        mkv = _matmul(memp, pk["mem_wkv"])
        mk, mv = mkv[:, :d].reshape(bp, mlen, d), mkv[:, d:].reshape(bp, mlen, d)
        x1 = _ffn_ln(yp, *pk["ffn"][0], pk["ln_g"][0], pk["ln_b"][0])
        proj = _matmul(x1, pk["w_in"])
        oa, oc, gla_s, rw_s = _mixers_recurrent(
            proj, bp, tp, pk, seg, jnp.zeros((bp, GLA_HEADS, GLA_DK, GLA_DV), F32),
            jnp.zeros((bp, RW_HEADS, RW_N, RW_N), F32), jnp.zeros((bp, RW_COLS), F32), None)
        bias_rows = jnp.repeat(pk["sb_bias"].reshape(SB_HEADS // 2, 2), LANES, axis=1)[..., None]
        ob = _sb_prompt(proj, bp, tp, bias_rows, mo)
        x2 = _merge_ln(x1, oa, ob, oc, proj, pk["b_gate"], pk["w_br"], pk["w_o"], pk["ln_g"][1], pk["ln_b"][1])
        x3 = _memattn_ln(x2, pk["mem_wq"], pk["mem_wo"], mk, mv, pk["ln_g"][2], pk["ln_b"][2], tp)
        yp = _ffn_ln(x3, *pk["ffn"][1], pk["ln_g"][3], pk["ln_b"][3])
        proj3 = proj.reshape(bp, tp, P_COLS)
        outs["kp"].append(proj3[:, :, COL_SB_K:COL_SB_K + SB_W].reshape(bp, tp, SB_HEADS, SB_DH))
        outs["vp"].append(proj3[:, :, COL_SB_V:COL_SB_V + SB_W].reshape(bp, tp, SB_HEADS, SB_DH))
        outs["gp"].append(gla_s)
        outs["rp"].append(rw_s)
        outs["shp"].append(proj3[:, -1, COL_RW:COL_RW + RW_COLS])
        outs["mk"].append(mk.reshape(bp, mlen, MEM_HEADS, d // MEM_HEADS))
        outs["mv"].append(mv.reshape(bp, mlen, MEM_HEADS, d // MEM_HEADS))
        x1 = _ffn_ln(ys, *pk["ffn"][0], pk["ln_g"][0], pk["ln_b"][0])
        proj = _matmul(x1, pk["w_in"])
        proj3 = proj.reshape(db, ts, P_COLS)
        proj_pad = jnp.pad(proj3, ((0, 0), (0, CHUNK - ts), (0, 0))).reshape(db * CHUNK, P_COLS)
        oa, oc, gla_s, rw_s = _mixers_recurrent(proj_pad, db, CHUNK, pk, seg, state_gla[l], state_rwkv[l],
                                                state_rwkv_shift[l], ts)
        oa = oa.reshape(db, CHUNK, GLA_V)[:, :ts].reshape(db * ts, GLA_V)
        oc = oc.reshape(db, CHUNK, RW_W)[:, :ts].reshape(db * ts, RW_W)
        qs = proj3[:, :, COL_SB_Q:COL_SB_Q + SB_W]
        ks_new = proj3[:, :, COL_SB_K:COL_SB_K + SB_W]
        vs_new = proj3[:, :, COL_SB_V:COL_SB_V + SB_W]
        qbd = (qs[:, :, None, :] * head_lane[None, None]).reshape(db, ts * SB_HEADS, SB_W)
        pad_page = ((0, 0), (0, PAGE_SIZE - ts), (0, 0))
        bias_col = jnp.tile(pk["sb_bias"], ts).reshape(ts * SB_HEADS, 1)
        ob = _sb_sample(qbd, bias_col, jnp.pad(ks_new, pad_page), jnp.pad(vs_new, pad_page), cache_k, cache_v,
                        page_table, l, mo, ts).reshape(db * ts, SB_W)
        x2 = _merge_ln(x1, oa, ob, oc, proj, pk["b_gate"], pk["w_br"], pk["w_o"], pk["ln_g"][1], pk["ln_b"][1])
        x3 = _memattn_ln(x2, pk["mem_wq"], pk["mem_wo"], cache_mem_k[l].reshape(db, mlen, d),
                         cache_mem_v[l].reshape(db, mlen, d), pk["ln_g"][2], pk["ln_b"][2], ts)
        ys = _ffn_ln(x3, *pk["ffn"][1], pk["ln_g"][3], pk["ln_b"][3])
        outs["ks"].append(ks_new.reshape(db, ts, SB_HEADS, SB_DH))
        outs["vs"].append(vs_new.reshape(db, ts, SB_HEADS, SB_DH))
        outs["gs"].append(gla_s)
        outs["rs"].append(rw_s)
        outs["shs"].append(proj3[:, -1, COL_RW:COL_RW + RW_COLS])
    st = lambda k: jnp.stack(outs[k])
    return (yp.reshape(bp, tp, d), ys.reshape(db, ts, d), st("kp"), st("vp"), st("gp"), st("rp"), st("shp"),
            st("mk"), st("mv"), st("ks"), st("vs"), st("gs"), st("rs"), st("shs"))
```

```python
import functools
import math

import jax
import jax.numpy as jnp
from jax import lax
from jax.experimental import pallas as pl
from jax.experimental.pallas import tpu as pltpu

F32 = jnp.float32
BF16 = jnp.bfloat16

DEPTH = 4
DN_ALPHA = (2 * DEPTH) ** 0.25
LN_EPS = 1e-5
LOG2E = 1.4426950408889634
FFN_RES = 0.5
GLA_HEADS, GLA_DK, GLA_DV, GLA_LR = 4, 64, 128, 16
GLA_GATE_NORM = 16.0
SB_HEADS, SB_DH = 8, 64
SB_HEAD_BITS = 3
RW_HEADS, RW_N = 8, 64
RW_LR_W, RW_LR_A, RW_LR_G = 64, 64, 128
RW_LN_EPS = 64e-5
N_BRANCH = 3
MEM_HEADS = 4
PAGE_SIZE = 128

GLA_QK = GLA_HEADS * GLA_DK
GLA_V = GLA_HEADS * GLA_DV
SB_W = SB_HEADS * SB_DH
RW_W = RW_HEADS * RW_N
RW_LR = RW_LR_W + RW_LR_A + RW_LR_G
RW_COLS = 3 * RW_W + RW_LR

COL_GLA_Q, COL_GLA_K, COL_GLA_V, COL_GLA_G = 0, 256, 512, 1024
COL_SB_Q, COL_SB_K, COL_SB_V = 1536, 2048, 2560
COL_GATE = 3072
COL_RW = 6144
COL_RW_LR = COL_RW + 3 * RW_W
COL_GDA = COL_RW + RW_COLS
P_COLS = 8192

LANES = 128
CHUNK = 64
VMEM_LIMIT = 56 * 1024 * 1024


def _cparams(sem):
    return pltpu.CompilerParams(dimension_semantics=sem, vmem_limit_bytes=VMEM_LIMIT)


def _iota(shape, dim):
    return lax.broadcasted_iota(jnp.int32, shape, dim)


_DN = {"nn": (((1,), (0,)), ((), ())), "nt": (((1,), (1,)), ((), ())), "tn": (((0,), (0,)), ((), ()))}


def _pdot(a, b, kind="nn"):
    return lax.dot_general(a.astype(BF16), b.astype(BF16), _DN[kind], preferred_element_type=F32)


def _split3(x):
    h = x.astype(BF16)
    r = x - h.astype(F32)
    m = r.astype(BF16)
    lo = (r - m.astype(F32)).astype(BF16)
    return h, m, lo


def _exact_left(mat_bf16, x):
    h, m, lo = _split3(x)
    d = lambda y: jnp.dot(mat_bf16, y, preferred_element_type=F32)
    return d(h) + (d(m) + d(lo))


def _exact_right(x, mat_bf16, terms=3):
    h, m, lo = _split3(x)
    d = lambda y: jnp.dot(y, mat_bf16, preferred_element_type=F32)
    if terms == 2:
        return d(h) + d(m)
    return d(h) + (d(m) + d(lo))


def _seg_sum(x, seg_bf16):
    w = seg_bf16.shape[0]
    parts = [_exact_right(x[:, j * w:(j + 1) * w], seg_bf16, terms=2) for j in range(x.shape[1] // w)]
    return jnp.concatenate(parts, axis=1)


def _layer_norm(y, g, b):
    mu = jnp.mean(y, axis=-1, keepdims=True)
    d = y - mu
    var = jnp.mean(d * d, axis=-1, keepdims=True)
    return d * lax.rsqrt(var + LN_EPS) * g + b


def _log_sigmoid(x):
    return jnp.minimum(x, 0.0) - jnp.log1p(jnp.exp(-jnp.abs(x)))


def _ffn_ln_body(x_ref, w1_ref, w3_ref, w2_ref, g_ref, b_ref, o_ref, xb_ref, acc_ref):
    j = pl.program_id(1)

    @pl.when(j == 0)
    def _():
        xb_ref[...] = x_ref[...].astype(BF16)
        acc_ref[...] = jnp.zeros_like(acc_ref)

    xb = xb_ref[...]
    h1 = jnp.dot(xb, w1_ref[...], preferred_element_type=F32)
    h3 = jnp.dot(xb, w3_ref[...], preferred_element_type=F32)
    h = (h1 * jax.nn.sigmoid(h1)) * h3
    acc_ref[...] += jnp.dot(h.astype(BF16), w2_ref[...], preferred_element_type=F32)

    @pl.when(j == pl.num_programs(1) - 1)
    def _():
        y = DN_ALPHA * x_ref[...] + FFN_RES * acc_ref[...]
        o_ref[...] = _layer_norm(y, g_ref[...], b_ref[...])


def _ffn_ln(x, w1, w3, w2, g, b):
    n, d = x.shape
    dff = w1.shape[1]
    tm = min(n, 1024)
    tf = 256
    assert n % tm == 0 and dff % tf == 0
    return pl.pallas_call(
        _ffn_ln_body,
        grid=(n // tm, dff // tf),
        in_specs=[
            pl.BlockSpec((tm, d), lambda i, j: (i, 0)),
            pl.BlockSpec((d, tf), lambda i, j: (0, j)),
            pl.BlockSpec((d, tf), lambda i, j: (0, j)),
            pl.BlockSpec((tf, d), lambda i, j: (j, 0)),
            pl.BlockSpec((1, d), lambda i, j: (0, 0)),
            pl.BlockSpec((1, d), lambda i, j: (0, 0)),
        ],
        out_specs=pl.BlockSpec((tm, d), lambda i, j: (i, 0)),
        out_shape=jax.ShapeDtypeStruct((n, d), F32),
        scratch_shapes=[pltpu.VMEM((tm, d), BF16), pltpu.VMEM((tm, d), F32)],
        compiler_params=_cparams(("parallel", "arbitrary")),
        name="ffn_ln",
    )(x, w1, w3, w2, g, b)


def _mm_body(x_ref, w_ref, o_ref, xb_ref):
    @pl.when(pl.program_id(1) == 0)
    def _():
        xb_ref[...] = x_ref[...].astype(BF16)

    o_ref[...] = jnp.dot(xb_ref[...], w_ref[...], preferred_element_type=F32)


def _matmul(x, w):
    n, k = x.shape
    m = w.shape[1]
    tm = min(n, 1024)
    tn = min(m, 1024)
    assert n % tm == 0 and m % tn == 0
    return pl.pallas_call(
        _mm_body,
        grid=(n // tm, m // tn),
        in_specs=[
            pl.BlockSpec((tm, k), lambda i, j: (i, 0)),
            pl.BlockSpec((k, tn), lambda i, j: (0, j)),
        ],
        out_specs=pl.BlockSpec((tm, tn), lambda i, j: (i, j)),
        out_shape=jax.ShapeDtypeStruct((n, m), F32),
        scratch_shapes=[pltpu.VMEM((tm, k), BF16)],
        compiler_params=_cparams(("parallel", "arbitrary")),
        name="proj_matmul",
    )(x, w)


def _merge_body(x_ref, oa_ref, ob_ref, oc_ref, pg0_ref, pg1_ref, pg2_ref, bg_ref, wbr_ref, wo_ref,
                g_ref, b_ref, o_ref):
    s = None
    for n, (o_r, pg_r) in enumerate(((oa_ref, pg0_ref), (ob_ref, pg1_ref), (oc_ref, pg2_ref))):
        br = jnp.dot(o_r[...].astype(BF16), wbr_ref[n], preferred_element_type=F32)
        term = jax.nn.sigmoid(pg_r[...] + bg_ref[n]) * br
        s = term if s is None else s + term
    mix = jnp.dot(s.astype(BF16), wo_ref[...], preferred_element_type=F32)
    o_ref[...] = _layer_norm(DN_ALPHA * x_ref[...] + mix, g_ref[...], b_ref[...])


def _merge_ln(x, oa, ob, oc, proj, b_gate, w_br, w_o, g, b):
    n, d = x.shape
    bw = oa.shape[1]
    tm = min(n, 512)
    gate_blk = COL_GATE // d
    row = lambda i: (i, 0)
    return pl.pallas_call(
        _merge_body,
        grid=(n // tm,),
        in_specs=[
            pl.BlockSpec((tm, d), row),
            pl.BlockSpec((tm, bw), row),
            pl.BlockSpec((tm, bw), row),
            pl.BlockSpec((tm, bw), row),
            pl.BlockSpec((tm, d), lambda i: (i, gate_blk)),
            pl.BlockSpec((tm, d), lambda i: (i, gate_blk + 1)),
            pl.BlockSpec((tm, d), lambda i: (i, gate_blk + 2)),
            pl.BlockSpec((N_BRANCH, 1, d), lambda i: (0, 0, 0)),
            pl.BlockSpec((N_BRANCH, bw, d), lambda i: (0, 0, 0)),
            pl.BlockSpec((d, d), lambda i: (0, 0)),
            pl.BlockSpec((1, d), lambda i: (0, 0)),
            pl.BlockSpec((1, d), lambda i: (0, 0)),
        ],
        out_specs=pl.BlockSpec((tm, d), row),
        out_shape=jax.ShapeDtypeStruct((n, d), F32),
        compiler_params=_cparams(("parallel",)),
        name="merge_ln",
    )(x, oa, ob, oc, proj, proj, proj, b_gate, w_br, w_o, g, b)


def _memattn_body(x_ref, wq_ref, wo_ref, mk_ref, mv_ref, g_ref, b_ref, o_ref, q_ref, acc_ref, *,
                  t_seq, tiles_per_batch, masked):
    i = pl.program_id(0)
    j = pl.program_id(1)
    tm, d = x_ref.shape
    dh = d // MEM_HEADS

    @pl.when(j == 0)
    def _():
        q_ref[...] = jnp.dot(x_ref[...].astype(BF16), wq_ref[...], preferred_element_type=F32).astype(BF16)
        acc_ref[...] = jnp.zeros_like(acc_ref)

    if masked:
        row = i * tm + _iota((tm, 1), 0)
        mem_b = i // tiles_per_batch + j
        keep = (row >= mem_b * t_seq) & (row < (mem_b + 1) * t_seq)
    for h in range(MEM_HEADS):
        sl = slice(h * dh, (h + 1) * dh)
        qh = q_ref[:, sl]
        kh = mk_ref[0, :, sl].astype(BF16)
        vh = mv_ref[0, :, sl].astype(BF16)
        s = lax.dot_general(qh, kh, _DN["nt"], preferred_element_type=F32) * (dh ** -0.5)
        e = jnp.exp(s - jnp.max(s, axis=-1, keepdims=True))
        att = e / jnp.sum(e, axis=-1, keepdims=True)
        oh = jnp.dot(att.astype(BF16), vh, preferred_element_type=F32)
        if masked:
            oh = jnp.where(keep, oh, 0.0)
        acc_ref[:, sl] += oh

    @pl.when(j == pl.num_programs(1) - 1)
    def _():
        xm = jnp.dot(acc_ref[...].astype(BF16), wo_ref[...], preferred_element_type=F32)
        o_ref[...] = _layer_norm(DN_ALPHA * x_ref[...] + xm, g_ref[...], b_ref[...])


def _memattn_ln(x, wq, wo, mem_k, mem_v, g, b, t_seq):
    n, d = x.shape
    nb, m, _ = mem_k.shape
    tm = min(n, 512)
    if t_seq >= tm:
        assert t_seq % tm == 0
        tiles_per_batch, nj, masked = t_seq // tm, 1, False
    else:
        assert tm % t_seq == 0 and n == tm
        tiles_per_batch, nj, masked = 1, nb, True
    body = functools.partial(_memattn_body, t_seq=t_seq, tiles_per_batch=tiles_per_batch, masked=masked)
    mem_map = lambda i, j: (i // tiles_per_batch + j, 0, 0)
    return pl.pallas_call(
        body,
        grid=(n // tm, nj),
        in_specs=[
            pl.BlockSpec((tm, d), lambda i, j: (i, 0)),
            pl.BlockSpec((d, d), lambda i, j: (0, 0)),
            pl.BlockSpec((d, d), lambda i, j: (0, 0)),
            pl.BlockSpec((1, m, d), mem_map),
            pl.BlockSpec((1, m, d), mem_map),
            pl.BlockSpec((1, d), lambda i, j: (0, 0)),
            pl.BlockSpec((1, d), lambda i, j: (0, 0)),
        ],
        out_specs=pl.BlockSpec((tm, d), lambda i, j: (i, 0)),
        out_shape=jax.ShapeDtypeStruct((n, d), F32),
        scratch_shapes=[pltpu.VMEM((tm, d), BF16), pltpu.VMEM((tm, d), F32)],
        compiler_params=_cparams(("parallel", "arbitrary")),
        name="memattn_ln",
    )(x, wq, wo, mem_k, mem_v, g, b)


def _stack2(x):
    return jnp.concatenate([x, x], axis=0)


def _head_rows_mask(c, width, seg):
    r = _iota((2 * c, width), 0)
    l = _iota((2 * c, width), 1)
    return ((r < c) & (l < seg)) | ((r >= c) & (l >= seg))


def _gla_body(q_ref, k_ref, v_ref, g_ref, gda_ref, wg2_ref, bg_ref, ng_ref, s0_ref, o_ref, sout_ref, s_ref, *,
              t_valid):
    t = pl.program_id(1)
    tt = q_ref.shape[0]
    c = CHUNK
    npair = GLA_HEADS // 2
    kw, vw = 2 * GLA_DK, 2 * GLA_DV

    @pl.when(t == 0)
    def _():
        s_ref[...] = s0_ref[0]

    ri = _iota((c, c), 0)
    ci = _iota((c, c), 1)
    tri_incl = (ci <= ri).astype(F32).astype(BF16)
    r2 = _iota((2 * c, 2 * c), 0)
    c2 = _iota((2 * c, 2 * c), 1)
    same_head = ((r2 < c) & (c2 < c)) | ((r2 >= c) & (c2 >= c))
    att_mask = same_head & ((c2 & (c - 1)) <= (r2 & (c - 1)))
    qmask = _head_rows_mask(c, kw, GLA_DK)
    omask = _head_rows_mask(c, vw, GLA_DV)
    sr = _iota((vw, kw), 0)
    sc = _iota((vw, kw), 1)
    bd_mask = ((sr < GLA_DV) & (sc < GLA_DK)) | ((sr >= GLA_DV) & (sc >= GLA_DK))

    def chunk(ic, carry):
        rows = pl.ds(pl.multiple_of(ic * c, c), c)
        la = _log_sigmoid(_pdot(gda_ref[rows, :], wg2_ref[...]) + bg_ref[...]) / GLA_GATE_NORM
        kk = k_ref[rows, :]
        if t_valid is not None:
            valid = (t * tt + ic * c + _iota((c, 1), 0)) < t_valid
            la = jnp.where(valid, la, 0.0)
            kk = jnp.where(valid, kk, 0.0)
        bc = _exact_left(tri_incl, la)
        b_last = bc[c - 1:c, :]
        q_dec = q_ref[rows, :] * (GLA_DK ** -0.5) * jnp.exp(bc)
        k_inv = kk * jnp.exp(-bc)
        k_end = kk * jnp.exp(b_last - bc)
        e_last = jnp.exp(b_last)
        vv = v_ref[rows, :]
        outs = []
        for p in range(npair):
            kl = slice(p * kw, (p + 1) * kw)
            vl = slice(p * vw, (p + 1) * vw)
            sp = s_ref[p]
            qd2 = jnp.where(qmask, _stack2(q_dec[:, kl]), 0.0)
            att = jnp.where(att_mask, _pdot(qd2, _stack2(k_inv[:, kl]), "nt"), 0.0)
            vp = vv[:, vl]
            o2 = _pdot(qd2, sp, "nt") + jnp.where(omask, _pdot(att, _stack2(vp)), 0.0)
            outs.append(o2[:c] + o2[c:])
            s_ref[p] = sp * e_last[:, kl] + jnp.where(bd_mask, _pdot(vp, k_end[:, kl], "tn"), 0.0)
        normed = []
        for p in range(npair):
            for h in range(2):
                oh = outs[p][:, h * GLA_DV:(h + 1) * GLA_DV]
                normed.append(oh * lax.rsqrt(jnp.mean(oh * oh, axis=-1, keepdims=True) + LN_EPS))
        o = jnp.concatenate(normed, axis=1) * ng_ref[...]
        gg = g_ref[rows, :]
        o_ref[rows, :] = o * (gg * jax.nn.sigmoid(gg))
        return carry

    lax.fori_loop(0, tt // c, chunk, 0)

    @pl.when(t == pl.num_programs(1) - 1)
    def _():
        sout_ref[0] = s_ref[...]


def _gla(proj, nb, t_len, wg2p, bg, ng, s0bd, t_valid):
    tt = min(t_len, 512)
    nt = t_len // tt
    npair = GLA_HEADS // 2
    kw, vw = 2 * GLA_DK, 2 * GLA_DV
    body = functools.partial(_gla_body, t_valid=t_valid)

    def col(width, off):
        blk = off // width
        return pl.BlockSpec((tt, width), lambda b, t: (b * nt + t, blk))

    const2 = lambda b, t: (0, 0)
    return pl.pallas_call(
        body,
        grid=(nb, nt),
        in_specs=[
            col(GLA_QK, COL_GLA_Q), col(GLA_QK, COL_GLA_K), col(GLA_V, COL_GLA_V), col(GLA_V, COL_GLA_G),
            col(LANES, COL_GDA),
            pl.BlockSpec((LANES, GLA_QK), const2),
            pl.BlockSpec((1, GLA_QK), const2),
            pl.BlockSpec((1, GLA_V), const2),
            pl.BlockSpec((1, npair, vw, kw), lambda b, t: (b, 0, 0, 0)),
        ],
        out_specs=[
            pl.BlockSpec((tt, GLA_V), lambda b, t: (b * nt + t, 0)),
            pl.BlockSpec((1, npair, vw, kw), lambda b, t: (b, 0, 0, 0)),
        ],
        out_shape=[
            jax.ShapeDtypeStruct((nb * t_len, GLA_V), F32),
            jax.ShapeDtypeStruct((nb, npair, vw, kw), F32),
        ],
        scratch_shapes=[pltpu.VMEM((npair, vw, kw), F32)],
        compiler_params=_cparams(("parallel", "arbitrary")),
        name="gla",
    )(proj, proj, proj, proj, proj, wg2p, bg, ng, s0bd)


def _rwkv_body(r_ref, k_ref, v_ref, lr_ref, shr_ref, shk_ref, shv_ref, shlr_ref, mur_ref, muk_ref, muv_ref,
               mulr_ref, w0_ref, w2_ref, a0_ref, a2_ref, g2_ref, kkp_ref, ka_ref, rk_ref, lng_ref, lnb_ref,
               seg_ref, s0_ref, o_ref, sout_ref,
               s_ref, cr_ref, ck_ref, cv_ref, clr_ref, rs_ref, ws_ref, ks_ref, vs_ref, as_ref, bs_ref, ys_ref, *,
               t_valid):
    t = pl.program_id(1)
    tt = r_ref.shape[0]
    c = CHUNK
    npair = RW_HEADS // 2
    pw = 2 * RW_N

    @pl.when(t == 0)
    def _():
        s_ref[...] = s0_ref[0]
        cr_ref[...] = shr_ref[0]
        ck_ref[...] = shk_ref[0]
        cv_ref[...] = shv_ref[0]
        clr_ref[...] = shlr_ref[0]

    def lerp(x_ref, carry_ref, mu_ref):
        x = x_ref[...]
        prev = pltpu.roll(x, 1, 0)
        prev = jnp.where(_iota(x.shape, 0) == 0, carry_ref[...], prev)
        carry_ref[...] = x_ref[pl.ds(tt - 1, 1), :]
        return x + (prev - x) * mu_ref[...]

    xr = lerp(r_ref, cr_ref, mur_ref)
    xk = lerp(k_ref, ck_ref, muk_ref)
    xv = lerp(v_ref, cv_ref, muv_ref)
    xlr = lerp(lr_ref, clr_ref, mulr_ref)
    w_log = _log_sigmoid(w0_ref[...] + _pdot(jnp.tanh(xlr), w2_ref[...])) - 0.5
    wdec = -jnp.exp(w_log)
    a = jax.nn.sigmoid(a0_ref[...] + _pdot(xlr, a2_ref[...]))
    g = _pdot(jax.nn.sigmoid(xlr), g2_ref[...])
    kk = xk * kkp_ref[...]
    kkn = kk / jnp.maximum(jnp.sqrt(_seg_sum(kk * kk, seg_ref[...])), 1e-12)
    kmod = xk * (1.0 + (a - 1.0) * ka_ref[...])
    av = -kkn
    bv = kkn * a
    if t_valid is not None:
        valid = (t * tt + _iota((tt, 1), 0)) < t_valid
        wdec = jnp.where(valid, wdec, 0.0)
        av = jnp.where(valid, av, 0.0)
        bv = jnp.where(valid, bv, 0.0)
        kmod = jnp.where(valid, kmod, 0.0)
        xv = jnp.where(valid, xv, 0.0)
    rs_ref[...] = xr
    ws_ref[...] = wdec
    ks_ref[...] = kmod
    vs_ref[...] = xv
    as_ref[...] = av
    bs_ref[...] = bv

    ri = _iota((c, c), 0)
    ci = _iota((c, c), 1)
    tri_incl = (ci <= ri).astype(F32).astype(BF16)
    r2 = _iota((2 * c, 2 * c), 0)
    c2 = _iota((2 * c, 2 * c), 1)
    same_head = ((r2 < c) & (c2 < c)) | ((r2 >= c) & (c2 >= c))
    strict = same_head & ((c2 & (c - 1)) < (r2 & (c - 1)))
    incl = same_head & ((c2 & (c - 1)) <= (r2 & (c - 1)))
    eye = (r2 == c2).astype(F32)
    hmask = _head_rows_mask(c, pw, RW_N)
    bd_mask = _head_rows_mask(RW_N, pw, RW_N)

    def chunk(ic, carry):
        rows = pl.ds(pl.multiple_of(ic * c, c), c)
        w = ws_ref[rows, :]
        cs = _exact_left(tri_incl, w)
        c_last = cs[c - 1:c, :]
        e_neg = jnp.exp(-cs)
        e_end = jnp.exp(c_last - cs)
        e_last = jnp.exp(c_last)
        a_t = as_ref[rows, :] * jnp.exp(cs - w)
        r_t = rs_ref[rows, :] * jnp.exp(cs)
        bb = bs_ref[rows, :]
        kc = ks_ref[rows, :]
        b_t = bb * e_neg
        k_t = kc * e_neg
        b_e = bb * e_end
        k_e = kc * e_end
        vv = vs_ref[rows, :]
        ys = []
        for p in range(npair):
            ln = slice(p * pw, (p + 1) * pw)
            sp = s_ref[p]
            ar4 = jnp.concatenate([jnp.where(hmask, _stack2(a_t[:, ln]), 0.0),
                                   jnp.where(hmask, _stack2(r_t[:, ln]), 0.0)], axis=0).astype(BF16)
            bk4 = jnp.concatenate([_stack2(b_t[:, ln]), _stack2(k_t[:, ln])], axis=0)
            v2 = _stack2(vv[:, ln])
            g4 = _pdot(ar4, bk4, "nt")
            n_ab = jnp.where(strict, g4[:2 * c, :2 * c], 0.0)
            n_ak = jnp.where(strict, g4[:2 * c, 2 * c:], 0.0)
            n_r = jnp.concatenate([jnp.where(incl, g4[2 * c:, :2 * c], 0.0),
                                   jnp.where(incl, g4[2 * c:, 2 * c:], 0.0)], axis=1)
            tinv = eye + n_ab
            x = n_ab
            for _ in range(int(math.log2(c)) - 1):
                x = _pdot(x, x)
                tinv = tinv + _pdot(tinv, x)
            as4 = _pdot(ar4, sp, "nt")
            w2 = as4[:2 * c] + jnp.where(hmask, _pdot(n_ak, v2), 0.0)
            u2 = _pdot(tinv, w2)
            y2 = as4[2 * c:] + jnp.where(hmask, _pdot(n_r, jnp.concatenate([u2, v2], axis=0)), 0.0)
            ys.append(y2[:c] + y2[c:])
            uv = jnp.concatenate([u2[:c] + u2[c:], vv[:, ln]], axis=0)
            upd = _pdot(uv, jnp.concatenate([b_e[:, ln], k_e[:, ln]], axis=0), "tn")
            s_ref[p] = sp * e_last[:, ln] + jnp.where(bd_mask, upd, 0.0)
        ys_ref[rows, :] = jnp.concatenate(ys, axis=1)
        return carry

    lax.fori_loop(0, tt // c, chunk, 0)

    y = ys_ref[...]
    seg = seg_ref[...]
    inv_n = 1.0 / RW_N
    mu = _seg_sum(y, seg) * inv_n
    d = y - mu
    var = _seg_sum(d * d, seg) * inv_n
    yn = d * lax.rsqrt(var + RW_LN_EPS) * lng_ref[...] + lnb_ref[...]
    bonus = _seg_sum(xr * kmod * rk_ref[...], seg)
    o_ref[...] = (yn + bonus * xv) * g

    @pl.when(t == pl.num_programs(1) - 1)
    def _():
        sout_ref[0] = s_ref[...]


def _rwkv(proj, nb, t_len, shifts, prm, seg, s0bd, t_valid):
    tt = min(t_len, 256)
    nt = t_len // tt
    npair = RW_HEADS // 2
    pw = 2 * RW_N
    body = functools.partial(_rwkv_body, t_valid=t_valid)

    def col(width, off):
        blk = off // width
        return pl.BlockSpec((tt, width), lambda b, t: (b * nt + t, blk))

    def per_batch(width):
        return pl.BlockSpec((1, 1, width), lambda b, t: (b, 0, 0))

    def const(shape):
        return pl.BlockSpec(shape, lambda b, t: (0,) * len(shape))

    vec = const((1, RW_W))
    lrm = const((RW_LR, RW_W))
    state = pl.BlockSpec((1, npair, pw, pw), lambda b, t: (b, 0, 0, 0))
    tile = lambda: pltpu.VMEM((tt, RW_W), F32)
    return pl.pallas_call(
        body,
        grid=(nb, nt),
        in_specs=[
            col(RW_W, COL_RW), col(RW_W, COL_RW + RW_W), col(RW_W, COL_RW + 2 * RW_W), col(RW_LR, COL_RW_LR),
            per_batch(RW_W), per_batch(RW_W), per_batch(RW_W), per_batch(RW_LR),
            vec, vec, vec, const((1, RW_LR)),
            vec, lrm, vec, lrm, lrm, vec, vec, vec, vec, vec,
            const(seg.shape),
            state,
        ],
        out_specs=[pl.BlockSpec((tt, RW_W), lambda b, t: (b * nt + t, 0)), state],
        out_shape=[
            jax.ShapeDtypeStruct((nb * t_len, RW_W), F32),
            jax.ShapeDtypeStruct((nb, npair, pw, pw), F32),
        ],
        scratch_shapes=[
            pltpu.VMEM((npair, pw, pw), F32),
            pltpu.VMEM((1, RW_W), F32), pltpu.VMEM((1, RW_W), F32), pltpu.VMEM((1, RW_W), F32),
            pltpu.VMEM((1, RW_LR), F32),
            tile(), tile(), tile(), tile(), tile(), tile(), tile(),
        ],
        compiler_params=_cparams(("parallel", "arbitrary")),
        name="rwkv7",
    )(proj, proj, proj, proj, *shifts, *prm, seg, s0bd)


SB_TQ = 512
SB_SUB = 256


def _sbp_body(q_ref, k_ref, v_ref, bias_ref, ms_ref, o_ref):
    i = pl.program_id(2)
    tq = q_ref.shape[0]
    tk = tq
    sub = min(SB_SUB, tk)
    nsub = tk // sub
    ms = ms_ref[...]
    lane = _iota((1, LANES), 1)
    q = q_ref[...] * (SB_DH ** -0.5 * LOG2E)
    causal = _iota((tq, tk), 1) < _iota((tq, tk), 0)
    out = jnp.zeros((tq, LANES), F32)
    for h in range(2):
        hm = (lane >= h * SB_DH) & (lane < (h + 1) * SB_DH)
        qh = jnp.where(hm, q, 0.0).astype(BF16)
        bias = bias_ref[0, h:h + 1, :] * LOG2E

        def block(kb, carry, masked, qh=qh, bias=bias):
            acc, later = carry
            rows = pl.ds(pl.multiple_of(kb * tk, tk), tk)
            kblk = k_ref[rows, :].astype(BF16)
            vblk = v_ref[rows, :].astype(BF16)
            z = lax.dot_general(qh, kblk, _DN["nt"], preferred_element_type=F32) + bias
            sp = jnp.maximum(z, 0.0) + jnp.log2(1.0 + jnp.exp2(-jnp.abs(z)))
            if masked:
                sp = jnp.where(causal, sp, 0.0)
            ws = [None] * nsub
            for j in reversed(range(nsub)):
                sl = slice(j * sub, (j + 1) * sub)
                spj = sp[:, sl]
                between = _exact_right(spj, ms, terms=2) + later
                ws[j] = jnp.exp2((z[:, sl] - spj) + between)
                later = later - jnp.sum(spj, axis=1, keepdims=True)
            w = ws[0] if nsub == 1 else jnp.concatenate(ws, axis=1)
            if masked:
                w = jnp.where(causal, w, 0.0)
            acc = acc + jnp.dot(w.astype(BF16), vblk, preferred_element_type=F32)
            return acc, later

        carry = block(i, (jnp.zeros((tq, LANES), F32), jnp.zeros((tq, 1), F32)), True)
        acc, _ = lax.fori_loop(0, i, lambda n, cr, blk=block: blk(i - 1 - n, cr, False), carry)
        out = out + jnp.where(hm, acc, 0.0)
    o_ref[...] = out


def _sb_prompt(proj, nb, t_len, bias_rows, ms):
    tq = min(SB_TQ, t_len)
    nq = t_len // tq
    npair = SB_HEADS // 2
    qb, kb, vb = COL_SB_Q // LANES, COL_SB_K // LANES, COL_SB_V // LANES
    return pl.pallas_call(
        _sbp_body,
        grid=(nb, npair, nq),
        in_specs=[
            pl.BlockSpec((tq, LANES), lambda b, p, i: (b * nq + i, qb + p)),
            pl.BlockSpec((t_len, LANES), lambda b, p, i: (b, kb + p)),
            pl.BlockSpec((t_len, LANES), lambda b, p, i: (b, vb + p)),
            pl.BlockSpec((1, 2, tq), lambda b, p, i: (p, 0, 0)),
            pl.BlockSpec(ms.shape, lambda b, p, i: (0, 0)),
        ],
        out_specs=pl.BlockSpec((tq, LANES), lambda b, p, i: (b * nq + i, p)),
        out_shape=jax.ShapeDtypeStruct((nb * t_len, SB_W), F32),
        compiler_params=_cparams(("parallel", "parallel", "arbitrary")),
        name="sb_prompt",
    )(proj, proj, proj, bias_rows, ms)


SB_PAGES_PER_STEP = 8


def _sbs_body(pt_ref, q_ref, bias_ref, knew_ref, vnew_ref, *rest):
    g_pages = SB_PAGES_PER_STEP
    k_refs = rest[:g_pages]
    v_refs = rest[g_pages:2 * g_pages]
    ms_ref, o_ref, acc_ref, later_ref = rest[2 * g_pages:]
    s = pl.program_id(1)
    nrow = q_ref.shape[1]
    ncol = knew_ref.shape[1]
    tpad = nrow // SB_HEADS
    sub = ms_ref.shape[0]
    q2 = (q_ref[0] * (SB_DH ** -0.5)).astype(BF16)
    bias = bias_ref[...]
    ms = ms_ref[...]
    row = _iota((nrow, ncol), 0)
    col = _iota((nrow, ncol), 1)
    own_head = (col & (SB_HEADS - 1)) * tpad == (row & -tpad)
    fresh_ok = own_head & ((col >> SB_HEAD_BITS) < (row & (tpad - 1)))

    subs = [slice(j * sub, (j + 1) * sub) for j in range(ncol // sub)]

    def page_terms(kblk, keep):
        z = lax.dot_general(q2, kblk.astype(BF16), _DN["nt"], preferred_element_type=F32) + bias
        sp = jnp.maximum(z, 0.0) + jnp.log1p(jnp.exp(-jnp.abs(z)))
        spm = jnp.where(keep, sp, 0.0)
        cums = [_exact_right(spm[:, sl], ms, terms=2) for sl in subs]
        tots = [jnp.sum(spm[:, sl], axis=1, keepdims=True) for sl in subs]
        return z - sp, cums, tots

    def combine(pages, keep, carry):
        acc, later = carry
        for (zs, cums, tots), vblk in pages:
            ws = [None] * len(subs)
            for j in reversed(range(len(subs))):
                ws[j] = jnp.exp(zs[:, subs[j]] + (cums[j] + later))
                later = later - tots[j]
            w = jnp.where(keep, jnp.concatenate(ws, axis=1), 0.0)
            acc = acc + jnp.dot(w.astype(BF16), vblk.astype(BF16), preferred_element_type=F32)
        return acc, later

    @pl.when(s == 0)
    def _():
        zero = (jnp.zeros(acc_ref.shape, F32), jnp.zeros(later_ref.shape, F32))
        acc_ref[...], later_ref[...] = combine([(page_terms(knew_ref[0], fresh_ok), vnew_ref[0])], fresh_ok, zero)

    @pl.when(s > 0)
    def _():
        pages = [(page_terms(k_refs[g][0], own_head), v_refs[g][0]) for g in range(g_pages)]
        acc_ref[...], later_ref[...] = combine(pages, own_head, (acc_ref[...], later_ref[...]))

    @pl.when(s == pl.num_programs(1) - 1)
    def _():
        o_ref[0] = acc_ref[...]


def _sb_sample(qh, bias_col, knew, vnew, cache_k, cache_v, page_table, layer, ms):
    nb, nrow, dh = qh.shape
    ncol = knew.shape[1]
    n_pages = page_table.shape[1]
    g_pages = SB_PAGES_PER_STEP
    assert n_pages % g_pages == 0
    nsteps = n_pages // g_pages
    base = layer * (cache_k.shape[0] // DEPTH)

    def page_spec(g):
        def imap(b, s, pt):
            page = n_pages - 1 - (jnp.maximum(s, 1) - 1) * g_pages - g
            return (base + pt[b, page], 0, 0)
        return pl.BlockSpec((1, ncol, dh), imap)

    per_b = lambda b, s, pt: (b, 0, 0)
    grid_spec = pltpu.PrefetchScalarGridSpec(
        num_scalar_prefetch=1,
        grid=(nb, nsteps + 1),
        in_specs=[
            pl.BlockSpec((1, nrow, dh), per_b),
            pl.BlockSpec((nrow, 1), lambda b, s, pt: (0, 0)),
            pl.BlockSpec((1, ncol, dh), per_b),
            pl.BlockSpec((1, ncol, dh), per_b),
            *[page_spec(g) for g in range(g_pages)],
            *[page_spec(g) for g in range(g_pages)],
            pl.BlockSpec(ms.shape, lambda b, s, pt: (0, 0)),
        ],
        out_specs=pl.BlockSpec((1, nrow, dh), per_b),
        scratch_shapes=[pltpu.VMEM((nrow, dh), F32), pltpu.VMEM((nrow, 1), F32)],
    )
    return pl.pallas_call(
        _sbs_body,
        grid_spec=grid_spec,
        out_shape=jax.ShapeDtypeStruct((nb, nrow, dh), F32),
        compiler_params=_cparams(("parallel", "arbitrary")),
        name="sb_sample",
    )(page_table, qh, bias_col, knew, vnew, *([cache_k] * g_pages), *([cache_v] * g_pages), ms)


def _blockdiag_in(s, pairs):
    b, h, do, di = s.shape
    s = s.reshape(b, pairs, 2, do, di)
    return jnp.einsum("bphvk,hg->bphvgk", s, jnp.eye(2, dtype=s.dtype)).reshape(b, pairs, 2 * do, 2 * di)


def _blockdiag_out(sbd, do, di):
    b, pairs = sbd.shape[:2]
    x = sbd.reshape(b, pairs, 2, do, 2, di)
    return jnp.einsum("bphvhk->bphvk", x).reshape(b, 2 * pairs, do, di)


def _pack_layer(l, ln_g, ln_b, ffn_w1, ffn_w3, ffn_w2, w_in, b_gate, gla_wg2, gla_bg, gla_norm_g, sb_bias, rw_mu,
                rw_w0, rw_w2, rw_a0, rw_a2, rw_g2, rw_kk, rw_ka, rw_rk, rw_lnx_g, rw_lnx_b, w_br, w_o, mem_wq,
                mem_wk, mem_wv, mem_wo):
    d = w_in.shape[1]
    gla_cols = 2 * GLA_QK + 2 * GLA_V + GLA_LR
    o_sb = gla_cols
    o_rw = o_sb + 3 * SB_W
    o_gate = o_rw + RW_COLS
    w = w_in[l]
    w_re = jnp.concatenate([
        w[:, :gla_cols - GLA_LR], w[:, o_sb:o_rw], w[:, o_gate:], w[:, o_rw:o_gate],
        w[:, gla_cols - GLA_LR:gla_cols], jnp.zeros((d, P_COLS - COL_GDA - GLA_LR), F32)], axis=1)
    assert w_re.shape[1] == P_COLS
    row = lambda v: v.reshape(1, -1)
    mu = rw_mu[l]

    def lr_pad(m, off):
        return jnp.zeros((RW_LR, RW_W), F32).at[off:off + m.shape[0]].set(m).astype(BF16)

    return dict(
        ln_g=[row(ln_g[l, i]) for i in range(4)], ln_b=[row(ln_b[l, i]) for i in range(4)],
        ffn=[(ffn_w1[l, i].astype(BF16), ffn_w3[l, i].astype(BF16), ffn_w2[l, i].astype(BF16)) for i in range(2)],
        w_in=w_re.astype(BF16),
        b_gate=b_gate[l].reshape(N_BRANCH, 1, -1), w_br=w_br[l].astype(BF16), w_o=w_o[l].astype(BF16),
        gla_wg2=jnp.zeros((LANES, GLA_QK), F32).at[:GLA_LR].set(gla_wg2[l]).astype(BF16),
        gla_bg=row(gla_bg[l]), gla_ng=row(jnp.tile(gla_norm_g[l], GLA_HEADS)),
        sb_bias=sb_bias[l],
        rw=[row(mu[:RW_W]), row(mu[RW_W:2 * RW_W]), row(mu[2 * RW_W:3 * RW_W]), row(mu[3 * RW_W:]),
            row(rw_w0[l]), lr_pad(rw_w2[l], 0), row(rw_a0[l]), lr_pad(rw_a2[l], RW_LR_W),
            lr_pad(rw_g2[l], RW_LR_W + RW_LR_A), row(rw_kk[l]), row(rw_ka[l]), row(rw_rk[l].reshape(-1)),
            row(rw_lnx_g[l]), row(rw_lnx_b[l])],
        mem_wq=mem_wq[l].astype(BF16), mem_wo=mem_wo[l].astype(BF16),
        mem_wkv=jnp.concatenate([mem_wk[l], mem_wv[l]], axis=1).astype(BF16),
    )


def _split_shift(sh):
    return [sh[:, None, :RW_W], sh[:, None, RW_W:2 * RW_W], sh[:, None, 2 * RW_W:3 * RW_W], sh[:, None, 3 * RW_W:]]


def _mixers_recurrent(proj, nb, t_len, pk, seg, gla_s0, rw_s0, rw_shift0, t_valid):
    oa, gla_s = _gla(proj, nb, t_len, pk["gla_wg2"], pk["gla_bg"], pk["gla_ng"],
                     _blockdiag_in(jnp.swapaxes(gla_s0, 2, 3), GLA_HEADS // 2), t_valid)
    oc, rw_s = _rwkv(proj, nb, t_len, _split_shift(rw_shift0), pk["rw"], seg,
                     _blockdiag_in(rw_s0, RW_HEADS // 2), t_valid)
    gla_s = jnp.swapaxes(_blockdiag_out(gla_s, GLA_DV, GLA_DK), 2, 3)
    rw_s = _blockdiag_out(rw_s, RW_N, RW_N)
    return oa, oc, gla_s, rw_s


def kernel(x_prompt, x_sample, mem_prompt, cache_sb_k, cache_sb_v, page_table, state_gla, state_rwkv,
           state_rwkv_shift, cache_mem_k, cache_mem_v, ln_g, ln_b, ffn_w1, ffn_w3, ffn_w2, w_in, b_gate,
           gla_wg2, gla_bg, gla_norm_g, sb_bias, rw_mu, rw_w0, rw_w2, rw_a0, rw_a2, rw_g2, rw_kk, rw_ka,
           rw_rk, rw_lnx_g, rw_lnx_b, w_br, w_o, mem_wq, mem_wk, mem_wv, mem_wo):
    bp, tp, d = x_prompt.shape
    db, ts, _ = x_sample.shape
    mlen = mem_prompt.shape[1]
    n_pool = cache_sb_k.shape[1]
    assert tp % CHUNK == 0 and ts <= CHUNK
    cache_k = cache_sb_k.reshape(DEPTH * n_pool, PAGE_SIZE * SB_HEADS, SB_DH)
    cache_v = cache_sb_v.reshape(DEPTH * n_pool, PAGE_SIZE * SB_HEADS, SB_DH)

    lane = jnp.arange(2 * LANES)
    seg = (lane[:, None] // RW_N == lane[None, :] // RW_N).astype(BF16)
    tq_sb = min(SB_TQ, tp)
    sidx = jnp.arange(min(SB_SUB, tq_sb))
    ms = -(sidx[:, None] > sidx[None, :]).astype(BF16)
    kidx = jnp.arange(SB_SUB) // SB_HEADS
    ms_s = -(kidx[:, None] > kidx[None, :]).astype(BF16)
    tpad = 1 << (ts - 1).bit_length()

    yp = x_prompt.reshape(bp * tp, d)
    ys = x_sample.reshape(db * ts, d)
    memp = mem_prompt.reshape(bp * mlen, d)
    outs = {k: [] for k in ("kp", "vp", "gp", "rp", "shp", "mk", "mv", "ks", "vs", "gs", "rs", "shs")}
    for l in range(DEPTH):
        pk = _pack_layer(l, ln_g, ln_b, ffn_w1, ffn_w3, ffn_w2, w_in, b_gate, gla_wg2, gla_bg, gla_norm_g, sb_bias,
                         rw_mu, rw_w0, rw_w2, rw_a0, rw_a2, rw_g2, rw_kk, rw_ka, rw_rk, rw_lnx_g, rw_lnx_b, w_br,
                         w_o, mem_wq, mem_wk, mem_wv, mem_wo)
        mkv = _matmul(memp, pk["mem_wkv"])
        mk, mv = mkv[:, :d].reshape(bp, mlen, d), mkv[:, d:].reshape(bp, mlen, d)
        x1 = _ffn_ln(yp, *pk["ffn"][0], pk["ln_g"][0], pk["ln_b"][0])
        proj = _matmul(x1, pk["w_in"])
        oa, oc, gla_s, rw_s = _mixers_recurrent(
            proj, bp, tp, pk, seg, jnp.zeros((bp, GLA_HEADS, GLA_DK, GLA_DV), F32),
            jnp.zeros((bp, RW_HEADS, RW_N, RW_N), F32), jnp.zeros((bp, RW_COLS), F32), None)
        bias_rows = jnp.broadcast_to(pk["sb_bias"].reshape(SB_HEADS // 2, 2, 1), (SB_HEADS // 2, 2, tq_sb))
        ob = _sb_prompt(proj, bp, tp, bias_rows, ms)
        x2 = _merge_ln(x1, oa, ob, oc, proj, pk["b_gate"], pk["w_br"], pk["w_o"], pk["ln_g"][1], pk["ln_b"][1])
        x3 = _memattn_ln(x2, pk["mem_wq"], pk["mem_wo"], mk, mv, pk["ln_g"][2], pk["ln_b"][2], tp)
        yp = _ffn_ln(x3, *pk["ffn"][1], pk["ln_g"][3], pk["ln_b"][3])
        proj3 = proj.reshape(bp, tp, P_COLS)
        outs["kp"].append(proj3[:, :, COL_SB_K:COL_SB_K + SB_W].reshape(bp, tp, SB_HEADS, SB_DH))
        outs["vp"].append(proj3[:, :, COL_SB_V:COL_SB_V + SB_W].reshape(bp, tp, SB_HEADS, SB_DH))
        outs["gp"].append(gla_s)
        outs["rp"].append(rw_s)
        outs["shp"].append(proj3[:, -1, COL_RW:COL_RW + RW_COLS])
        outs["mk"].append(mk.reshape(bp, mlen, MEM_HEADS, d // MEM_HEADS))
        outs["mv"].append(mv.reshape(bp, mlen, MEM_HEADS, d // MEM_HEADS))
        x1 = _ffn_ln(ys, *pk["ffn"][0], pk["ln_g"][0], pk["ln_b"][0])
        proj = _matmul(x1, pk["w_in"])
        proj3 = proj.reshape(db, ts, P_COLS)
        proj_pad = jnp.pad(proj3, ((0, 0), (0, CHUNK - ts), (0, 0))).reshape(db * CHUNK, P_COLS)
        oa, oc, gla_s, rw_s = _mixers_recurrent(proj_pad, db, CHUNK, pk, seg, state_gla[l], state_rwkv[l],
                                                state_rwkv_shift[l], ts)
        oa = oa.reshape(db, CHUNK, GLA_V)[:, :ts].reshape(db * ts, GLA_V)
        oc = oc.reshape(db, CHUNK, RW_W)[:, :ts].reshape(db * ts, RW_W)
        qs = proj3[:, :, COL_SB_Q:COL_SB_Q + SB_W]
        ks_new = proj3[:, :, COL_SB_K:COL_SB_K + SB_W]
        vs_new = proj3[:, :, COL_SB_V:COL_SB_V + SB_W]
        qh = jnp.pad(qs.reshape(db, ts, SB_HEADS, SB_DH).transpose(0, 2, 1, 3), ((0, 0), (0, 0), (0, tpad - ts), (0, 0)))
        pad_page = ((0, 0), (0, (PAGE_SIZE - ts) * SB_HEADS), (0, 0))
        bias_col = jnp.repeat(pk["sb_bias"], tpad).reshape(SB_HEADS * tpad, 1)
        ob = _sb_sample(qh.reshape(db, SB_HEADS * tpad, SB_DH), bias_col,
                        jnp.pad(ks_new.reshape(db, ts * SB_HEADS, SB_DH), pad_page),
                        jnp.pad(vs_new.reshape(db, ts * SB_HEADS, SB_DH), pad_page),
                        cache_k, cache_v, page_table, l, ms_s)
        ob = ob.reshape(db, SB_HEADS, tpad, SB_DH)[:, :, :ts].transpose(0, 2, 1, 3).reshape(db * ts, SB_W)
        x2 = _merge_ln(x1, oa, ob, oc, proj, pk["b_gate"], pk["w_br"], pk["w_o"], pk["ln_g"][1], pk["ln_b"][1])
        x3 = _memattn_ln(x2, pk["mem_wq"], pk["mem_wo"], cache_mem_k[l].reshape(db, mlen, d),
                         cache_mem_v[l].reshape(db, mlen, d), pk["ln_g"][2], pk["ln_b"][2], ts)
        ys = _ffn_ln(x3, *pk["ffn"][1], pk["ln_g"][3], pk["ln_b"][3])
        outs["ks"].append(ks_new.reshape(db, ts, SB_HEADS, SB_DH))
        outs["vs"].append(vs_new.reshape(db, ts, SB_HEADS, SB_DH))
        outs["gs"].append(gla_s)
        outs["rs"].append(rw_s)
        outs["shs"].append(proj3[:, -1, COL_RW:COL_RW + RW_COLS])
    st = lambda k: jnp.stack(outs[k])
    return (yp.reshape(bp, tp, d), ys.reshape(db, ts, d), st("kp"), st("vp"), st("gp"), st("rp"), st("shp"),
            st("mk"), st("mv"), st("ks"), st("vs"), st("gs"), st("rs"), st("shs"))
```

```python
import functools
import math

import jax
import jax.numpy as jnp
from jax import lax
from jax.experimental import pallas as pl
from jax.experimental.pallas import tpu as pltpu

F32 = jnp.float32
BF16 = jnp.bfloat16

DEPTH = 4
DN_ALPHA = (2 * DEPTH) ** 0.25
LN_EPS = 1e-5
LOG2E = 1.4426950408889634
FFN_RES = 0.5
GLA_HEADS, GLA_DK, GLA_DV, GLA_LR = 4, 64, 128, 16
GLA_GATE_NORM = 16.0
SB_HEADS, SB_DH = 8, 64
RW_HEADS, RW_N = 8, 64
RW_LR_W, RW_LR_A, RW_LR_G = 64, 64, 128
RW_LN_EPS = 64e-5
N_BRANCH = 3
MEM_HEADS = 4
PAGE_SIZE = 128

GLA_QK = GLA_HEADS * GLA_DK
GLA_V = GLA_HEADS * GLA_DV
SB_W = SB_HEADS * SB_DH
RW_W = RW_HEADS * RW_N
RW_LR = RW_LR_W + RW_LR_A + RW_LR_G
RW_COLS = 3 * RW_W + RW_LR

COL_GLA_Q, COL_GLA_K, COL_GLA_V, COL_GLA_G = 0, 256, 512, 1024
COL_SB_Q, COL_SB_K, COL_SB_V = 1536, 2048, 2560
COL_GATE = 3072
COL_RW = 6144
COL_RW_LR = COL_RW + 3 * RW_W
COL_GDA = COL_RW + RW_COLS
P_COLS = 8192

LANES = 128
CHUNK = 64
VMEM_LIMIT = 56 * 1024 * 1024


def _cparams(sem):
    return pltpu.CompilerParams(dimension_semantics=sem, vmem_limit_bytes=VMEM_LIMIT)


def _iota(shape, dim):
    return lax.broadcasted_iota(jnp.int32, shape, dim)


_DN = {"nn": (((1,), (0,)), ((), ())), "nt": (((1,), (1,)), ((), ())), "tn": (((0,), (0,)), ((), ()))}


def _pdot(a, b, kind="nn"):
    return lax.dot_general(a.astype(BF16), b.astype(BF16), _DN[kind], preferred_element_type=F32)


def _split3(x):
    h = x.astype(BF16)
    r = x - h.astype(F32)
    m = r.astype(BF16)
    lo = (r - m.astype(F32)).astype(BF16)
    return h, m, lo


def _exact_left(mat_bf16, x):
    h, m, lo = _split3(x)
    d = lambda y: jnp.dot(mat_bf16, y, preferred_element_type=F32)
    return d(h) + (d(m) + d(lo))


def _exact_right(x, mat_bf16, terms=3):
    h, m, lo = _split3(x)
    d = lambda y: jnp.dot(y, mat_bf16, preferred_element_type=F32)
    if terms == 2:
        return d(h) + d(m)
    return d(h) + (d(m) + d(lo))


def _exact_right2(x, mat2_bf16):
    h = x.astype(BF16)
    m = (x - h.astype(F32)).astype(BF16)
    return jnp.dot(jnp.concatenate([h, m], axis=1), mat2_bf16, preferred_element_type=F32)


def _seg_sum(x, seg_bf16):
    w = seg_bf16.shape[0]
    parts = [_exact_right(x[:, j * w:(j + 1) * w], seg_bf16, terms=2) for j in range(x.shape[1] // w)]
    return jnp.concatenate(parts, axis=1)


def _layer_norm(y, g, b):
    mu = jnp.mean(y, axis=-1, keepdims=True)
    d = y - mu
    var = jnp.mean(d * d, axis=-1, keepdims=True)
    return d * lax.rsqrt(var + LN_EPS) * g + b


def _log_sigmoid(x):
    return jnp.minimum(x, 0.0) - jnp.log1p(jnp.exp(-jnp.abs(x)))


def _ffn_ln_body(x_ref, w1_ref, w3_ref, w2_ref, g_ref, b_ref, o_ref, xb_ref, acc_ref):
    j = pl.program_id(1)

    @pl.when(j == 0)
    def _():
        xb_ref[...] = x_ref[...].astype(BF16)
        acc_ref[...] = jnp.zeros_like(acc_ref)

    xb = xb_ref[...]
    h1 = jnp.dot(xb, w1_ref[...], preferred_element_type=F32)
    h3 = jnp.dot(xb, w3_ref[...], preferred_element_type=F32)
    h = (h1 * jax.nn.sigmoid(h1)) * h3
    acc_ref[...] += jnp.dot(h.astype(BF16), w2_ref[...], preferred_element_type=F32)

    @pl.when(j == pl.num_programs(1) - 1)
    def _():
        y = DN_ALPHA * x_ref[...] + FFN_RES * acc_ref[...]
        o_ref[...] = _layer_norm(y, g_ref[...], b_ref[...])


def _ffn_ln(x, w1, w3, w2, g, b):
    n, d = x.shape
    dff = w1.shape[1]
    tm = min(n, 1024)
    tf = 256
    assert n % tm == 0 and dff % tf == 0
    return pl.pallas_call(
        _ffn_ln_body,
        grid=(n // tm, dff // tf),
        in_specs=[
            pl.BlockSpec((tm, d), lambda i, j: (i, 0)),
            pl.BlockSpec((d, tf), lambda i, j: (0, j)),
            pl.BlockSpec((d, tf), lambda i, j: (0, j)),
            pl.BlockSpec((tf, d), lambda i, j: (j, 0)),
            pl.BlockSpec((1, d), lambda i, j: (0, 0)),
            pl.BlockSpec((1, d), lambda i, j: (0, 0)),
        ],
        out_specs=pl.BlockSpec((tm, d), lambda i, j: (i, 0)),
        out_shape=jax.ShapeDtypeStruct((n, d), F32),
        scratch_shapes=[pltpu.VMEM((tm, d), BF16), pltpu.VMEM((tm, d), F32)],
        compiler_params=_cparams(("parallel", "arbitrary")),
        name="ffn_ln",
    )(x, w1, w3, w2, g, b)


def _mm_body(x_ref, w_ref, o_ref, xb_ref):
    @pl.when(pl.program_id(1) == 0)
    def _():
        xb_ref[...] = x_ref[...].astype(BF16)

    o_ref[...] = jnp.dot(xb_ref[...], w_ref[...], preferred_element_type=F32)


def _matmul(x, w):
    n, k = x.shape
    m = w.shape[1]
    tm = min(n, 1024)
    tn = min(m, 1024)
    assert n % tm == 0 and m % tn == 0
    return pl.pallas_call(
        _mm_body,
        grid=(n // tm, m // tn),
        in_specs=[
            pl.BlockSpec((tm, k), lambda i, j: (i, 0)),
            pl.BlockSpec((k, tn), lambda i, j: (0, j)),
        ],
        out_specs=pl.BlockSpec((tm, tn), lambda i, j: (i, j)),
        out_shape=jax.ShapeDtypeStruct((n, m), F32),
        scratch_shapes=[pltpu.VMEM((tm, k), BF16)],
        compiler_params=_cparams(("parallel", "arbitrary")),
        name="proj_matmul",
    )(x, w)


def _merge_body(x_ref, oa_ref, ob_ref, oc_ref, pg0_ref, pg1_ref, pg2_ref, bg_ref, wbr_ref, wo_ref,
                g_ref, b_ref, o_ref):
    s = None
    for n, (o_r, pg_r) in enumerate(((oa_ref, pg0_ref), (ob_ref, pg1_ref), (oc_ref, pg2_ref))):
        br = jnp.dot(o_r[...].astype(BF16), wbr_ref[n], preferred_element_type=F32)
        term = jax.nn.sigmoid(pg_r[...] + bg_ref[n]) * br
        s = term if s is None else s + term
    mix = jnp.dot(s.astype(BF16), wo_ref[...], preferred_element_type=F32)
    o_ref[...] = _layer_norm(DN_ALPHA * x_ref[...] + mix, g_ref[...], b_ref[...])


def _merge_ln(x, oa, ob, oc, proj, b_gate, w_br, w_o, g, b):
    n, d = x.shape
    bw = oa.shape[1]
    tm = min(n, 512)
    gate_blk = COL_GATE // d
    row = lambda i: (i, 0)
    return pl.pallas_call(
        _merge_body,
        grid=(n // tm,),
        in_specs=[
            pl.BlockSpec((tm, d), row),
            pl.BlockSpec((tm, bw), row),
            pl.BlockSpec((tm, bw), row),
            pl.BlockSpec((tm, bw), row),
            pl.BlockSpec((tm, d), lambda i: (i, gate_blk)),
            pl.BlockSpec((tm, d), lambda i: (i, gate_blk + 1)),
            pl.BlockSpec((tm, d), lambda i: (i, gate_blk + 2)),
            pl.BlockSpec((N_BRANCH, 1, d), lambda i: (0, 0, 0)),
            pl.BlockSpec((N_BRANCH, bw, d), lambda i: (0, 0, 0)),
            pl.BlockSpec((d, d), lambda i: (0, 0)),
            pl.BlockSpec((1, d), lambda i: (0, 0)),
            pl.BlockSpec((1, d), lambda i: (0, 0)),
        ],
        out_specs=pl.BlockSpec((tm, d), row),
        out_shape=jax.ShapeDtypeStruct((n, d), F32),
        compiler_params=_cparams(("parallel",)),
        name="merge_ln",
    )(x, oa, ob, oc, proj, proj, proj, b_gate, w_br, w_o, g, b)


def _memattn_body(x_ref, wq_ref, wo_ref, mk_ref, mv_ref, g_ref, b_ref, o_ref, q_ref, acc_ref, *,
                  t_seq, tiles_per_batch, masked):
    i = pl.program_id(0)
    j = pl.program_id(1)
    tm, d = x_ref.shape
    dh = d // MEM_HEADS

    @pl.when(j == 0)
    def _():
        q_ref[...] = jnp.dot(x_ref[...].astype(BF16), wq_ref[...], preferred_element_type=F32).astype(BF16)
        acc_ref[...] = jnp.zeros_like(acc_ref)

    if masked:
        row = i * tm + _iota((tm, 1), 0)
        mem_b = i // tiles_per_batch + j
        keep = (row >= mem_b * t_seq) & (row < (mem_b + 1) * t_seq)
    for h in range(MEM_HEADS):
        sl = slice(h * dh, (h + 1) * dh)
        qh = q_ref[:, sl]
        kh = mk_ref[0, :, sl].astype(BF16)
        vh = mv_ref[0, :, sl].astype(BF16)
        s = lax.dot_general(qh, kh, _DN["nt"], preferred_element_type=F32) * (dh ** -0.5)
        e = jnp.exp(s - jnp.max(s, axis=-1, keepdims=True))
        att = e / jnp.sum(e, axis=-1, keepdims=True)
        oh = jnp.dot(att.astype(BF16), vh, preferred_element_type=F32)
        if masked:
            oh = jnp.where(keep, oh, 0.0)
        acc_ref[:, sl] += oh

    @pl.when(j == pl.num_programs(1) - 1)
    def _():
        xm = jnp.dot(acc_ref[...].astype(BF16), wo_ref[...], preferred_element_type=F32)
        o_ref[...] = _layer_norm(DN_ALPHA * x_ref[...] + xm, g_ref[...], b_ref[...])


def _memattn_ln(x, wq, wo, mem_k, mem_v, g, b, t_seq):
    n, d = x.shape
    nb, m, _ = mem_k.shape
    tm = min(n, 512)
    if t_seq >= tm:
        assert t_seq % tm == 0
        tiles_per_batch, nj, masked = t_seq // tm, 1, False
    else:
        assert tm % t_seq == 0 and n == tm
        tiles_per_batch, nj, masked = 1, nb, True
    body = functools.partial(_memattn_body, t_seq=t_seq, tiles_per_batch=tiles_per_batch, masked=masked)
    mem_map = lambda i, j: (i // tiles_per_batch + j, 0, 0)
    return pl.pallas_call(
        body,
        grid=(n // tm, nj),
        in_specs=[
            pl.BlockSpec((tm, d), lambda i, j: (i, 0)),
            pl.BlockSpec((d, d), lambda i, j: (0, 0)),
            pl.BlockSpec((d, d), lambda i, j: (0, 0)),
            pl.BlockSpec((1, m, d), mem_map),
            pl.BlockSpec((1, m, d), mem_map),
            pl.BlockSpec((1, d), lambda i, j: (0, 0)),
            pl.BlockSpec((1, d), lambda i, j: (0, 0)),
        ],
        out_specs=pl.BlockSpec((tm, d), lambda i, j: (i, 0)),
        out_shape=jax.ShapeDtypeStruct((n, d), F32),
        scratch_shapes=[pltpu.VMEM((tm, d), BF16), pltpu.VMEM((tm, d), F32)],
        compiler_params=_cparams(("parallel", "arbitrary")),
        name="memattn_ln",
    )(x, wq, wo, mem_k, mem_v, g, b)


def _stack2(x):
    return jnp.concatenate([x, x], axis=0)


def _head_rows_mask(c, width, seg):
    r = _iota((2 * c, width), 0)
    l = _iota((2 * c, width), 1)
    return ((r < c) & (l < seg)) | ((r >= c) & (l >= seg))


GLA_CHUNKS_PER_ITER = 4


def _gla_body(q_ref, k_ref, v_ref, g_ref, gda_ref, wg2_ref, bg_ref, ng_ref, s0_ref, o_ref, sout_ref, s_ref, *,
              t_valid):
    t = pl.program_id(1)
    tt = q_ref.shape[0]
    c = CHUNK
    npair = GLA_HEADS // 2
    kw, vw = 2 * GLA_DK, 2 * GLA_DV

    @pl.when(t == 0)
    def _():
        s_ref[...] = s0_ref[0]

    ri = _iota((c, c), 0)
    ci = _iota((c, c), 1)
    tri_incl = (ci <= ri).astype(F32).astype(BF16)
    r2 = _iota((2 * c, 2 * c), 0)
    c2 = _iota((2 * c, 2 * c), 1)
    same_head = ((r2 < c) & (c2 < c)) | ((r2 >= c) & (c2 >= c))
    att_mask = same_head & ((c2 & (c - 1)) <= (r2 & (c - 1)))
    qmask = _head_rows_mask(c, kw, GLA_DK)
    omask = _head_rows_mask(c, vw, GLA_DV)
    sr = _iota((vw, kw), 0)
    sc = _iota((vw, kw), 1)
    bd_mask = ((sr < GLA_DV) & (sc < GLA_DK)) | ((sr >= GLA_DV) & (sc >= GLA_DK))

    cpi = min(GLA_CHUNKS_PER_ITER, tt // c)
    pairs = range(npair)
    kls = [slice(p * kw, (p + 1) * kw) for p in pairs]
    vls = [slice(p * vw, (p + 1) * vw) for p in pairs]

    def chunk_group(ig, carry):
        units = [(j, p) for j in range(cpi) for p in pairs]
        grows = pl.ds(pl.multiple_of(ig * (cpi * c), cpi * c), cpi * c)
        rows = [pl.ds(pl.multiple_of((ig * cpi + j) * c, c), c) for j in range(cpi)]
        la = _log_sigmoid(_pdot(gda_ref[grows, :], wg2_ref[...]) + bg_ref[...]) / GLA_GATE_NORM
        kk = k_ref[grows, :]
        if t_valid is not None:
            valid = (t * tt + ig * (cpi * c) + _iota((cpi * c, 1), 0)) < t_valid
            la = jnp.where(valid, la, 0.0)
            kk = jnp.where(valid, kk, 0.0)
        la3 = _split3(la)
        bcs = [sum(jnp.dot(tri_incl, part[j * c:(j + 1) * c], preferred_element_type=F32) for part in la3)
               for j in range(cpi)]
        b_last = [bc[c - 1:c, :] for bc in bcs]
        e_last = [jnp.exp(b) for b in b_last]
        qs = q_ref[grows, :] * (GLA_DK ** -0.5)
        q_dec = [qs[j * c:(j + 1) * c] * jnp.exp(bcs[j]) for j in range(cpi)]
        k_inv = [kk[j * c:(j + 1) * c] * jnp.exp(-bcs[j]) for j in range(cpi)]
        k_end = [kk[j * c:(j + 1) * c] * jnp.exp(b_last[j] - bcs[j]) for j in range(cpi)]
        vv = [v_ref[rows[j], :] for j in range(cpi)]
        qd2 = {(j, p): jnp.where(qmask, _stack2(q_dec[j][:, kls[p]]), 0.0).astype(BF16) for j, p in units}
        att = {(j, p): jnp.where(att_mask, _pdot(qd2[j, p], _stack2(k_inv[j][:, kls[p]]), "nt"), 0.0)
               for j, p in units}
        intra = {(j, p): jnp.where(omask, _pdot(att[j, p], _stack2(vv[j][:, vls[p]])), 0.0) for j, p in units}
        s_cur = [s_ref[p] for p in pairs]
        for j in range(cpi):
            o2 = [_pdot(qd2[j, p], s_cur[p], "nt") + intra[j, p] for p in pairs]
            upd = [_pdot(vv[j][:, vls[p]], k_end[j][:, kls[p]], "tn") for p in pairs]
            s_cur = [s_cur[p] * e_last[j][:, kls[p]] + jnp.where(bd_mask, upd[p], 0.0) for p in pairs]
            normed = []
            for p in pairs:
                op = o2[p][:c] + o2[p][c:]
                for h in range(2):
                    oh = op[:, h * GLA_DV:(h + 1) * GLA_DV]
                    normed.append(oh * lax.rsqrt(jnp.mean(oh * oh, axis=-1, keepdims=True) + LN_EPS))
            o = jnp.concatenate(normed, axis=1) * ng_ref[...]
            gg = g_ref[rows[j], :]
            o_ref[rows[j], :] = o * (gg * jax.nn.sigmoid(gg))
        for p in pairs:
            s_ref[p] = s_cur[p]
        return carry

    lax.fori_loop(0, tt // (c * cpi), chunk_group, 0)

    @pl.when(t == pl.num_programs(1) - 1)
    def _():
        sout_ref[0] = s_ref[...]


def _gla(proj, nb, t_len, wg2p, bg, ng, s0bd, t_valid):
    tt = min(t_len, 512)
    nt = t_len // tt
    npair = GLA_HEADS // 2
    kw, vw = 2 * GLA_DK, 2 * GLA_DV
    body = functools.partial(_gla_body, t_valid=t_valid)

    def col(width, off):
        blk = off // width
        return pl.BlockSpec((tt, width), lambda b, t: (b * nt + t, blk))

    const2 = lambda b, t: (0, 0)
    return pl.pallas_call(
        body,
        grid=(nb, nt),
        in_specs=[
            col(GLA_QK, COL_GLA_Q), col(GLA_QK, COL_GLA_K), col(GLA_V, COL_GLA_V), col(GLA_V, COL_GLA_G),
            col(LANES, COL_GDA),
            pl.BlockSpec((LANES, GLA_QK), const2),
            pl.BlockSpec((1, GLA_QK), const2),
            pl.BlockSpec((1, GLA_V), const2),
            pl.BlockSpec((1, npair, vw, kw), lambda b, t: (b, 0, 0, 0)),
        ],
        out_specs=[
            pl.BlockSpec((tt, GLA_V), lambda b, t: (b * nt + t, 0)),
            pl.BlockSpec((1, npair, vw, kw), lambda b, t: (b, 0, 0, 0)),
        ],
        out_shape=[
            jax.ShapeDtypeStruct((nb * t_len, GLA_V), F32),
            jax.ShapeDtypeStruct((nb, npair, vw, kw), F32),
        ],
        scratch_shapes=[pltpu.VMEM((npair, vw, kw), F32)],
        compiler_params=_cparams(("parallel", "arbitrary")),
        name="gla",
    )(proj, proj, proj, proj, proj, wg2p, bg, ng, s0bd)


RW_CHUNKS_PER_ITER = 4


def _rwkv_body(r_ref, k_ref, v_ref, lr_ref, shr_ref, shk_ref, shv_ref, shlr_ref, mur_ref, muk_ref, muv_ref,
               mulr_ref, w0_ref, w2_ref, a0_ref, a2_ref, g2_ref, kkp_ref, ka_ref, rk_ref, lng_ref, lnb_ref,
               seg_ref, s0_ref, o_ref, sout_ref,
               s_ref, cr_ref, ck_ref, cv_ref, clr_ref, rs_ref, ws_ref, ks_ref, vs_ref, as_ref, bs_ref, ys_ref, *,
               t_valid):
    t = pl.program_id(1)
    tt = r_ref.shape[0]
    c = CHUNK
    npair = RW_HEADS // 2
    pw = 2 * RW_N

    @pl.when(t == 0)
    def _():
        s_ref[...] = s0_ref[0]
        cr_ref[...] = shr_ref[0]
        ck_ref[...] = shk_ref[0]
        cv_ref[...] = shv_ref[0]
        clr_ref[...] = shlr_ref[0]

    def lerp(x_ref, carry_ref, mu_ref):
        x = x_ref[...]
        prev = pltpu.roll(x, 1, 0)
        prev = jnp.where(_iota(x.shape, 0) == 0, carry_ref[...], prev)
        carry_ref[...] = x_ref[pl.ds(tt - 1, 1), :]
        return x + (prev - x) * mu_ref[...]

    xr = lerp(r_ref, cr_ref, mur_ref)
    xk = lerp(k_ref, ck_ref, muk_ref)
    xv = lerp(v_ref, cv_ref, muv_ref)
    xlr = lerp(lr_ref, clr_ref, mulr_ref)
    w_log = _log_sigmoid(w0_ref[...] + _pdot(jnp.tanh(xlr), w2_ref[...])) - 0.5
    wdec = -jnp.exp(w_log)
    a = jax.nn.sigmoid(a0_ref[...] + _pdot(xlr, a2_ref[...]))
    g = _pdot(jax.nn.sigmoid(xlr), g2_ref[...])
    kk = xk * kkp_ref[...]
    kkn = kk / jnp.maximum(jnp.sqrt(_seg_sum(kk * kk, seg_ref[...])), 1e-12)
    kmod = xk * (1.0 + (a - 1.0) * ka_ref[...])
    av = -kkn
    bv = kkn * a
    if t_valid is not None:
        valid = (t * tt + _iota((tt, 1), 0)) < t_valid
        wdec = jnp.where(valid, wdec, 0.0)
        av = jnp.where(valid, av, 0.0)
        bv = jnp.where(valid, bv, 0.0)
        kmod = jnp.where(valid, kmod, 0.0)
        xv = jnp.where(valid, xv, 0.0)
    rs_ref[...] = xr
    ws_ref[...] = wdec
    ks_ref[...] = kmod
    vs_ref[...] = xv
    as_ref[...] = av
    bs_ref[...] = bv

    ri = _iota((c, c), 0)
    ci = _iota((c, c), 1)
    tri_incl = (ci <= ri).astype(F32).astype(BF16)
    r2 = _iota((2 * c, 2 * c), 0)
    c2 = _iota((2 * c, 2 * c), 1)
    same_head = ((r2 < c) & (c2 < c)) | ((r2 >= c) & (c2 >= c))
    strict = same_head & ((c2 & (c - 1)) < (r2 & (c - 1)))
    incl = same_head & ((c2 & (c - 1)) <= (r2 & (c - 1)))
    eye = (r2 == c2).astype(F32)
    hmask = _head_rows_mask(c, pw, RW_N)
    bd_mask = _head_rows_mask(RW_N, pw, RW_N)

    cpi = min(RW_CHUNKS_PER_ITER, tt // c)
    pairs = range(npair)
    lanes = [slice(p * pw, (p + 1) * pw) for p in pairs]
    units = [(j, p) for j in range(cpi) for p in pairs]

    def chunk_group(ig, carry):
        rows, b_e, k_e, vv, e_last = [], [], [], [], []
        ar4, bk4, v2 = {}, {}, {}
        for j in range(cpi):
            rows.append(pl.ds(pl.multiple_of((ig * cpi + j) * c, c), c))
            w = ws_ref[rows[j], :]
            cs = _exact_left(tri_incl, w)
            c_last = cs[c - 1:c, :]
            e_neg = jnp.exp(-cs)
            e_end = jnp.exp(c_last - cs)
            e_last.append(jnp.exp(c_last))
            a_t = as_ref[rows[j], :] * jnp.exp(cs - w)
            r_t = rs_ref[rows[j], :] * jnp.exp(cs)
            bb = bs_ref[rows[j], :]
            kc = ks_ref[rows[j], :]
            b_t = bb * e_neg
            k_t = kc * e_neg
            b_e.append(bb * e_end)
            k_e.append(kc * e_end)
            vv.append(vs_ref[rows[j], :])
            for p in pairs:
                ln = lanes[p]
                ar4[j, p] = jnp.concatenate([jnp.where(hmask, _stack2(a_t[:, ln]), 0.0),
                                             jnp.where(hmask, _stack2(r_t[:, ln]), 0.0)], axis=0).astype(BF16)
                bk4[j, p] = jnp.concatenate([_stack2(b_t[:, ln]), _stack2(k_t[:, ln])], axis=0).astype(BF16)
                v2[j, p] = _stack2(vv[j][:, ln]).astype(BF16)
        g4 = {u: _pdot(ar4[u], bk4[u], "nt") for u in units}
        n_ab = {u: jnp.where(strict, g4[u][:2 * c, :2 * c], 0.0) for u in units}
        n_ak = {u: jnp.where(strict, g4[u][:2 * c, 2 * c:], 0.0) for u in units}
        n_r = {u: jnp.concatenate([jnp.where(incl, g4[u][2 * c:, :2 * c], 0.0),
                                   jnp.where(incl, g4[u][2 * c:, 2 * c:], 0.0)], axis=1).astype(BF16)
               for u in units}
        nakv = {u: jnp.where(hmask, _pdot(n_ak[u], v2[u]), 0.0) for u in units}
        tinv = {u: eye + n_ab[u] for u in units}
        x = n_ab
        for _ in range(int(math.log2(c)) - 1):
            x = {u: _pdot(x[u], x[u]) for u in units}
            tinv = {u: tinv[u] + _pdot(tinv[u], x[u]) for u in units}
        s_cur = [s_ref[p] for p in pairs]
        for j in range(cpi):
            as4 = [_pdot(ar4[j, p], s_cur[p], "nt") for p in pairs]
            u2 = [_pdot(tinv[j, p], as4[p][:2 * c] + nakv[j, p]) for p in pairs]
            y2 = [as4[p][2 * c:] + jnp.where(
                hmask, _pdot(n_r[j, p], jnp.concatenate([u2[p].astype(BF16), v2[j, p]], axis=0)), 0.0)
                  for p in pairs]
            upd = [_pdot(jnp.concatenate([u2[p][:c] + u2[p][c:], vv[j][:, lanes[p]]], axis=0),
                         jnp.concatenate([b_e[j][:, lanes[p]], k_e[j][:, lanes[p]]], axis=0), "tn") for p in pairs]
            s_cur = [s_cur[p] * e_last[j][:, lanes[p]] + jnp.where(bd_mask, upd[p], 0.0) for p in pairs]
            ys_ref[rows[j], :] = jnp.concatenate([y2[p][:c] + y2[p][c:] for p in pairs], axis=1)
        for p in pairs:
            s_ref[p] = s_cur[p]
        return carry

    lax.fori_loop(0, tt // (c * cpi), chunk_group, 0)

    y = ys_ref[...]
    seg = seg_ref[...]
    inv_n = 1.0 / RW_N
    mu = _seg_sum(y, seg) * inv_n
    d = y - mu
    var = _seg_sum(d * d, seg) * inv_n
    yn = d * lax.rsqrt(var + RW_LN_EPS) * lng_ref[...] + lnb_ref[...]
    bonus = _seg_sum(xr * kmod * rk_ref[...], seg)
    o_ref[...] = (yn + bonus * xv) * g

    @pl.when(t == pl.num_programs(1) - 1)
    def _():
        sout_ref[0] = s_ref[...]


def _rwkv(proj, nb, t_len, shifts, prm, seg, s0bd, t_valid):
    tt = min(t_len, 256)
    nt = t_len // tt
    npair = RW_HEADS // 2
    pw = 2 * RW_N
    body = functools.partial(_rwkv_body, t_valid=t_valid)

    def col(width, off):
        blk = off // width
        return pl.BlockSpec((tt, width), lambda b, t: (b * nt + t, blk))

    def per_batch(width):
        return pl.BlockSpec((1, 1, width), lambda b, t: (b, 0, 0))

    def const(shape):
        return pl.BlockSpec(shape, lambda b, t: (0,) * len(shape))

    vec = const((1, RW_W))
    lrm = const((RW_LR, RW_W))
    state = pl.BlockSpec((1, npair, pw, pw), lambda b, t: (b, 0, 0, 0))
    tile = lambda: pltpu.VMEM((tt, RW_W), F32)
    return pl.pallas_call(
        body,
        grid=(nb, nt),
        in_specs=[
            col(RW_W, COL_RW), col(RW_W, COL_RW + RW_W), col(RW_W, COL_RW + 2 * RW_W), col(RW_LR, COL_RW_LR),
            per_batch(RW_W), per_batch(RW_W), per_batch(RW_W), per_batch(RW_LR),
            vec, vec, vec, const((1, RW_LR)),
            vec, lrm, vec, lrm, lrm, vec, vec, vec, vec, vec,
            const(seg.shape),
            state,
        ],
        out_specs=[pl.BlockSpec((tt, RW_W), lambda b, t: (b * nt + t, 0)), state],
        out_shape=[
            jax.ShapeDtypeStruct((nb * t_len, RW_W), F32),
            jax.ShapeDtypeStruct((nb, npair, pw, pw), F32),
        ],
        scratch_shapes=[
            pltpu.VMEM((npair, pw, pw), F32),
            pltpu.VMEM((1, RW_W), F32), pltpu.VMEM((1, RW_W), F32), pltpu.VMEM((1, RW_W), F32),
            pltpu.VMEM((1, RW_LR), F32),
            tile(), tile(), tile(), tile(), tile(), tile(), tile(),
        ],
        compiler_params=_cparams(("parallel", "arbitrary")),
        name="rwkv7",
    )(proj, proj, proj, proj, *shifts, *prm, seg, s0bd)


SB_TQ = 512
SB_SUB = 256


def _sbp_body(q_ref, k_ref, v_ref, bias_ref, ms_ref, o_ref):
    i = pl.program_id(2)
    tq = q_ref.shape[0]
    tk = tq
    sub = min(SB_SUB, tk)
    nsub = tk // sub
    ms = ms_ref[...]
    lane = _iota((1, LANES), 1)
    q = q_ref[...] * (SB_DH ** -0.5 * LOG2E)
    causal = _iota((tq, tk), 1) < _iota((tq, tk), 0)
    out = jnp.zeros((tq, LANES), F32)
    for h in range(2):
        hm = (lane >= h * SB_DH) & (lane < (h + 1) * SB_DH)
        qh = jnp.where(hm, q, 0.0).astype(BF16)
        bias = bias_ref[0, h:h + 1, :] * LOG2E

        def block(kb, carry, masked, qh=qh, bias=bias):
            acc, later = carry
            rows = pl.ds(pl.multiple_of(kb * tk, tk), tk)
            kblk = k_ref[rows, :].astype(BF16)
            vblk = v_ref[rows, :].astype(BF16)
            z = lax.dot_general(qh, kblk, _DN["nt"], preferred_element_type=F32) + bias
            sp = jnp.maximum(z, 0.0) + jnp.log2(1.0 + jnp.exp2(-jnp.abs(z)))
            if masked:
                sp = jnp.where(causal, sp, 0.0)
            ws = [None] * nsub
            for j in reversed(range(nsub)):
                sl = slice(j * sub, (j + 1) * sub)
                spj = sp[:, sl]
                between = _exact_right2(spj, ms) + later
                ws[j] = jnp.exp2((z[:, sl] - spj) + between)
                later = later - jnp.sum(spj, axis=1, keepdims=True)
            w = ws[0] if nsub == 1 else jnp.concatenate(ws, axis=1)
            if masked:
                w = jnp.where(causal, w, 0.0)
            acc = acc + jnp.dot(w.astype(BF16), vblk, preferred_element_type=F32)
            return acc, later

        carry = block(i, (jnp.zeros((tq, LANES), F32), jnp.zeros((tq, 1), F32)), True)
        acc, _ = lax.fori_loop(0, i, lambda n, cr, blk=block: blk(i - 1 - n, cr, False), carry)
        out = out + jnp.where(hm, acc, 0.0)
    o_ref[...] = out


def _sb_prompt(proj, nb, t_len, bias_rows, ms):
    tq = min(SB_TQ, t_len)
    nq = t_len // tq
    npair = SB_HEADS // 2
    qb, kb, vb = COL_SB_Q // LANES, COL_SB_K // LANES, COL_SB_V // LANES
    return pl.pallas_call(
        _sbp_body,
        grid=(nb, npair, nq),
        in_specs=[
            pl.BlockSpec((tq, LANES), lambda b, p, i: (b * nq + i, qb + p)),
            pl.BlockSpec((t_len, LANES), lambda b, p, i: (b, kb + p)),
            pl.BlockSpec((t_len, LANES), lambda b, p, i: (b, vb + p)),
            pl.BlockSpec((1, 2, tq), lambda b, p, i: (p, 0, 0)),
            pl.BlockSpec(ms.shape, lambda b, p, i: (0, 0)),
        ],
        out_specs=pl.BlockSpec((tq, LANES), lambda b, p, i: (b * nq + i, p)),
        out_shape=jax.ShapeDtypeStruct((nb * t_len, SB_W), F32),
        compiler_params=_cparams(("parallel", "parallel", "arbitrary")),
        name="sb_prompt",
    )(proj, proj, proj, bias_rows, ms)


SB_PAGES_PER_STEP = 8


def _sbs_body(pt_ref, q_ref, bias_ref, knew_ref, vnew_ref, *rest):
    g_pages = SB_PAGES_PER_STEP
    k_refs = rest[:g_pages]
    v_refs = rest[g_pages:2 * g_pages]
    ms_ref, o_ref, acc_ref, later_ref = rest[2 * g_pages:]
    s = pl.program_id(1)
    nrow, width = q_ref.shape[1:]
    nkey = knew_ref.shape[2]
    tpad = nrow // SB_HEADS
    q2 = (q_ref[0] * (SB_DH ** -0.5)).astype(BF16)
    bias = bias_ref[...]
    ms = ms_ref[...]
    fresh_ok = _iota((nrow, nkey), 1) < (_iota((nrow, nkey), 0) & (tpad - 1))

    def page_terms(kt, keep):
        z = jnp.dot(q2, kt.astype(BF16), preferred_element_type=F32) + bias
        sp = jnp.maximum(z, 0.0) + jnp.log1p(jnp.exp(-jnp.abs(z)))
        zs = z - sp
        if keep is not None:
            sp = jnp.where(keep, sp, 0.0)
        return zs, _exact_right(sp, ms, terms=2), jnp.sum(sp, axis=1, keepdims=True)

    def combine(pages, keep, carry):
        acc, later = carry
        for (zs, cum, tot), vt in pages:
            w = jnp.exp(zs + (cum + later))
            later = later - tot
            if keep is not None:
                w = jnp.where(keep, w, 0.0)
            acc = acc + lax.dot_general(w.astype(BF16), vt.astype(BF16), _DN["nt"], preferred_element_type=F32)
        return acc, later

    @pl.when(s == 0)
    def _():
        zero = (jnp.zeros(acc_ref.shape, F32), jnp.zeros(later_ref.shape, F32))
        acc_ref[...], later_ref[...] = combine([(page_terms(knew_ref[0], fresh_ok), vnew_ref[0])], fresh_ok, zero)

    @pl.when(s > 0)
    def _():
        flat = lambda ref: ref[0, 0].reshape(width, nkey)
        pages = [(page_terms(flat(k_refs[g]), None), flat(v_refs[g])) for g in range(g_pages)]
        acc_ref[...], later_ref[...] = combine(pages, None, (acc_ref[...], later_ref[...]))

    @pl.when(s == pl.num_programs(1) - 1)
    def _():
        o_ref[0] = acc_ref[...]


def _sb_sample(qbd, bias_col, knew_t, vnew_t, cache_kt, cache_vt, page_table, layer, ms):
    nb, nrow, width = qbd.shape
    nkey = knew_t.shape[2]
    n_pages = page_table.shape[1]
    g_pages = SB_PAGES_PER_STEP
    assert n_pages % g_pages == 0
    nsteps = n_pages // g_pages

    def page_spec(g):
        def imap(b, s, pt):
            page = n_pages - 1 - (jnp.maximum(s, 1) - 1) * g_pages - g
            return (layer, pt[b, page], 0, 0, 0)
        return pl.BlockSpec((1, 1) + cache_kt.shape[2:], imap)

    per_b = lambda b, s, pt: (b, 0, 0)
    grid_spec = pltpu.PrefetchScalarGridSpec(
        num_scalar_prefetch=1,
        grid=(nb, nsteps + 1),
        in_specs=[
            pl.BlockSpec((1, nrow, width), per_b),
            pl.BlockSpec((nrow, 1), lambda b, s, pt: (0, 0)),
            pl.BlockSpec((1, width, nkey), per_b),
            pl.BlockSpec((1, width, nkey), per_b),
            *[page_spec(g) for g in range(g_pages)],
            *[page_spec(g) for g in range(g_pages)],
            pl.BlockSpec(ms.shape, lambda b, s, pt: (0, 0)),
        ],
        out_specs=pl.BlockSpec((1, nrow, width), per_b),
        scratch_shapes=[pltpu.VMEM((nrow, width), F32), pltpu.VMEM((nrow, 1), F32)],
    )
    return pl.pallas_call(
        _sbs_body,
        grid_spec=grid_spec,
        out_shape=jax.ShapeDtypeStruct((nb, nrow, width), F32),
        compiler_params=_cparams(("parallel", "arbitrary")),
        name="sb_sample",
    )(page_table, qbd, bias_col, knew_t, vnew_t, *([cache_kt] * g_pages), *([cache_vt] * g_pages), ms)


def _blockdiag_in(s, pairs):
    b, h, do, di = s.shape
    s = s.reshape(b, pairs, 2, do, di)
    return jnp.einsum("bphvk,hg->bphvgk", s, jnp.eye(2, dtype=s.dtype)).reshape(b, pairs, 2 * do, 2 * di)


def _blockdiag_out(sbd, do, di):
    b, pairs = sbd.shape[:2]
    x = sbd.reshape(b, pairs, 2, do, 2, di)
    return jnp.einsum("bphvhk->bphvk", x).reshape(b, 2 * pairs, do, di)


def _pack_layer(l, ln_g, ln_b, ffn_w1, ffn_w3, ffn_w2, w_in, b_gate, gla_wg2, gla_bg, gla_norm_g, sb_bias, rw_mu,
                rw_w0, rw_w2, rw_a0, rw_a2, rw_g2, rw_kk, rw_ka, rw_rk, rw_lnx_g, rw_lnx_b, w_br, w_o, mem_wq,
                mem_wk, mem_wv, mem_wo):
    d = w_in.shape[1]
    gla_cols = 2 * GLA_QK + 2 * GLA_V + GLA_LR
    o_sb = gla_cols
    o_rw = o_sb + 3 * SB_W
    o_gate = o_rw + RW_COLS
    w = w_in[l]
    w_re = jnp.concatenate([
        w[:, :gla_cols - GLA_LR], w[:, o_sb:o_rw], w[:, o_gate:], w[:, o_rw:o_gate],
        w[:, gla_cols - GLA_LR:gla_cols], jnp.zeros((d, P_COLS - COL_GDA - GLA_LR), F32)], axis=1)
    assert w_re.shape[1] == P_COLS
    row = lambda v: v.reshape(1, -1)
    mu = rw_mu[l]

    def lr_pad(m, off):
        return jnp.zeros((RW_LR, RW_W), F32).at[off:off + m.shape[0]].set(m).astype(BF16)

    return dict(
        ln_g=[row(ln_g[l, i]) for i in range(4)], ln_b=[row(ln_b[l, i]) for i in range(4)],
        ffn=[(ffn_w1[l, i].astype(BF16), ffn_w3[l, i].astype(BF16), ffn_w2[l, i].astype(BF16)) for i in range(2)],
        w_in=w_re.astype(BF16),
        b_gate=b_gate[l].reshape(N_BRANCH, 1, -1), w_br=w_br[l].astype(BF16), w_o=w_o[l].astype(BF16),
        gla_wg2=jnp.zeros((LANES, GLA_QK), F32).at[:GLA_LR].set(gla_wg2[l]).astype(BF16),
        gla_bg=row(gla_bg[l]), gla_ng=row(jnp.tile(gla_norm_g[l], GLA_HEADS)),
        sb_bias=sb_bias[l],
        rw=[row(mu[:RW_W]), row(mu[RW_W:2 * RW_W]), row(mu[2 * RW_W:3 * RW_W]), row(mu[3 * RW_W:]),
            row(rw_w0[l]), lr_pad(rw_w2[l], 0), row(rw_a0[l]), lr_pad(rw_a2[l], RW_LR_W),
            lr_pad(rw_g2[l], RW_LR_W + RW_LR_A), row(rw_kk[l]), row(rw_ka[l]), row(rw_rk[l].reshape(-1)),
            row(rw_lnx_g[l]), row(rw_lnx_b[l])],
        mem_wq=mem_wq[l].astype(BF16), mem_wo=mem_wo[l].astype(BF16),
        mem_wkv=jnp.concatenate([mem_wk[l], mem_wv[l]], axis=1).astype(BF16),
    )


def _split_shift(sh):
    return [sh[:, None, :RW_W], sh[:, None, RW_W:2 * RW_W], sh[:, None, 2 * RW_W:3 * RW_W], sh[:, None, 3 * RW_W:]]


def _mixers_recurrent(proj, nb, t_len, pk, seg, gla_s0, rw_s0, rw_shift0, t_valid):
    oa, gla_s = _gla(proj, nb, t_len, pk["gla_wg2"], pk["gla_bg"], pk["gla_ng"],
                     _blockdiag_in(jnp.swapaxes(gla_s0, 2, 3), GLA_HEADS // 2), t_valid)
    oc, rw_s = _rwkv(proj, nb, t_len, _split_shift(rw_shift0), pk["rw"], seg,
                     _blockdiag_in(rw_s0, RW_HEADS // 2), t_valid)
    gla_s = jnp.swapaxes(_blockdiag_out(gla_s, GLA_DV, GLA_DK), 2, 3)
    rw_s = _blockdiag_out(rw_s, RW_N, RW_N)
    return oa, oc, gla_s, rw_s


def kernel(x_prompt, x_sample, mem_prompt, cache_sb_k, cache_sb_v, page_table, state_gla, state_rwkv,
           state_rwkv_shift, cache_mem_k, cache_mem_v, ln_g, ln_b, ffn_w1, ffn_w3, ffn_w2, w_in, b_gate,
           gla_wg2, gla_bg, gla_norm_g, sb_bias, rw_mu, rw_w0, rw_w2, rw_a0, rw_a2, rw_g2, rw_kk, rw_ka,
           rw_rk, rw_lnx_g, rw_lnx_b, w_br, w_o, mem_wq, mem_wk, mem_wv, mem_wo):
    bp, tp, d = x_prompt.shape
    db, ts, _ = x_sample.shape
    mlen = mem_prompt.shape[1]
    n_pool = cache_sb_k.shape[1]
    assert tp % CHUNK == 0 and ts <= CHUNK

    lane = jnp.arange(2 * LANES)
    seg = (lane[:, None] // RW_N == lane[None, :] // RW_N).astype(BF16)
    tq_sb = min(SB_TQ, tp)
    sidx = jnp.arange(min(SB_SUB, tq_sb))
    ms = -(sidx[:, None] > sidx[None, :]).astype(BF16)
    ms = jnp.concatenate([ms, ms], axis=0)
    kidx = jnp.arange(PAGE_SIZE)
    ms_s = -(kidx[:, None] > kidx[None, :]).astype(BF16)
    head_lane = (jnp.arange(SB_W)[None, :] // SB_DH == jnp.arange(SB_HEADS)[:, None]).astype(F32)
    cache_kt = cache_sb_k.transpose(0, 1, 3, 4, 2)
    cache_vt = cache_sb_v.transpose(0, 1, 3, 4, 2)
    tpad = 1 << (ts - 1).bit_length()

    yp = x_prompt.reshape(bp * tp, d)
    ys = x_sample.reshape(db * ts, d)
    memp = mem_prompt.reshape(bp * mlen, d)
    outs = {k: [] for k in ("kp", "vp", "gp", "rp", "shp", "mk", "mv", "ks", "vs", "gs", "rs", "shs")}
    for l in range(DEPTH):
        pk = _pack_layer(l, ln_g, ln_b, ffn_w1, ffn_w3, ffn_w2, w_in, b_gate, gla_wg2, gla_bg, gla_norm_g, sb_bias,
                         rw_mu, rw_w0, rw_w2, rw_a0, rw_a2, rw_g2, rw_kk, rw_ka, rw_rk, rw_lnx_g, rw_lnx_b, w_br,
                         w_o, mem_wq, mem_wk, mem_wv, mem_wo)
        mkv = _matmul(memp, pk["mem_wkv"])
        mk, mv = mkv[:, :d].reshape(bp, mlen, d), mkv[:, d:].reshape(bp, mlen, d)
        x1 = _ffn_ln(yp, *pk["ffn"][0], pk["ln_g"][0], pk["ln_b"][0])
        proj = _matmul(x1, pk["w_in"])
        oa, oc, gla_s, rw_s = _mixers_recurrent(
            proj, bp, tp, pk, seg, jnp.zeros((bp, GLA_HEADS, GLA_DK, GLA_DV), F32),
            jnp.zeros((bp, RW_HEADS, RW_N, RW_N), F32), jnp.zeros((bp, RW_COLS), F32), None)
        bias_rows = jnp.broadcast_to(pk["sb_bias"].reshape(SB_HEADS // 2, 2, 1), (SB_HEADS // 2, 2, tq_sb))
        ob = _sb_prompt(proj, bp, tp, bias_rows, ms)
        x2 = _merge_ln(x1, oa, ob, oc, proj, pk["b_gate"], pk["w_br"], pk["w_o"], pk["ln_g"][1], pk["ln_b"][1])
        x3 = _memattn_ln(x2, pk["mem_wq"], pk["mem_wo"], mk, mv, pk["ln_g"][2], pk["ln_b"][2], tp)
        yp = _ffn_ln(x3, *pk["ffn"][1], pk["ln_g"][3], pk["ln_b"][3])
        proj3 = proj.reshape(bp, tp, P_COLS)
        outs["kp"].append(proj3[:, :, COL_SB_K:COL_SB_K + SB_W].reshape(bp, tp, SB_HEADS, SB_DH))
        outs["vp"].append(proj3[:, :, COL_SB_V:COL_SB_V + SB_W].reshape(bp, tp, SB_HEADS, SB_DH))
        outs["gp"].append(gla_s)
        outs["rp"].append(rw_s)
        outs["shp"].append(proj3[:, -1, COL_RW:COL_RW + RW_COLS])
        outs["mk"].append(mk.reshape(bp, mlen, MEM_HEADS, d // MEM_HEADS))
        outs["mv"].append(mv.reshape(bp, mlen, MEM_HEADS, d // MEM_HEADS))
        x1 = _ffn_ln(ys, *pk["ffn"][0], pk["ln_g"][0], pk["ln_b"][0])
        proj = _matmul(x1, pk["w_in"])
        proj3 = proj.reshape(db, ts, P_COLS)
        proj_pad = jnp.pad(proj3, ((0, 0), (0, CHUNK - ts), (0, 0))).reshape(db * CHUNK, P_COLS)
        oa, oc, gla_s, rw_s = _mixers_recurrent(proj_pad, db, CHUNK, pk, seg, state_gla[l], state_rwkv[l],
                                                state_rwkv_shift[l], ts)
        oa = oa.reshape(db, CHUNK, GLA_V)[:, :ts].reshape(db * ts, GLA_V)
        oc = oc.reshape(db, CHUNK, RW_W)[:, :ts].reshape(db * ts, RW_W)
        qs = proj3[:, :, COL_SB_Q:COL_SB_Q + SB_W]
        ks_new = proj3[:, :, COL_SB_K:COL_SB_K + SB_W]
        vs_new = proj3[:, :, COL_SB_V:COL_SB_V + SB_W]
        qpad = jnp.pad(qs, ((0, 0), (0, tpad - ts), (0, 0)))
        qbd = (qpad[:, None, :, :] * head_lane[None, :, None, :]).reshape(db, SB_HEADS * tpad, SB_W)
        pad_page = ((0, 0), (0, 0), (0, PAGE_SIZE - ts))
        bias_col = jnp.repeat(pk["sb_bias"], tpad).reshape(SB_HEADS * tpad, 1)
        ob = _sb_sample(qbd, bias_col, jnp.pad(jnp.swapaxes(ks_new, 1, 2), pad_page),
                        jnp.pad(jnp.swapaxes(vs_new, 1, 2), pad_page), cache_kt, cache_vt, page_table, l, ms_s)
        ob = jnp.einsum("bhtgd,hg->btgd", ob.reshape(db, SB_HEADS, tpad, SB_HEADS, SB_DH)[:, :, :ts],
                        jnp.eye(SB_HEADS, dtype=F32)).reshape(db * ts, SB_W)
        x2 = _merge_ln(x1, oa, ob, oc, proj, pk["b_gate"], pk["w_br"], pk["w_o"], pk["ln_g"][1], pk["ln_b"][1])
        x3 = _memattn_ln(x2, pk["mem_wq"], pk["mem_wo"], cache_mem_k[l].reshape(db, mlen, d),
                         cache_mem_v[l].reshape(db, mlen, d), pk["ln_g"][2], pk["ln_b"][2], ts)
        ys = _ffn_ln(x3, *pk["ffn"][1], pk["ln_g"][3], pk["ln_b"][3])
        outs["ks"].append(ks_new.reshape(db, ts, SB_HEADS, SB_DH))
        outs["vs"].append(vs_new.reshape(db, ts, SB_HEADS, SB_DH))
        outs["gs"].append(gla_s)
        outs["rs"].append(rw_s)
        outs["shs"].append(proj3[:, -1, COL_RW:COL_RW + RW_COLS])
    st = lambda k: jnp.stack(outs[k])
    return (yp.reshape(bp, tp, d), ys.reshape(db, ts, d), st("kp"), st("vp"), st("gp"), st("rp"), st("shp"),
            st("mk"), st("mv"), st("ks"), st("vs"), st("gs"), st("rs"), st("shs"))
```

```python
import functools
import math

import jax
import jax.numpy as jnp
from jax import lax
from jax.experimental import pallas as pl
from jax.experimental.pallas import tpu as pltpu

F32 = jnp.float32
BF16 = jnp.bfloat16

DEPTH = 4
DN_ALPHA = (2 * DEPTH) ** 0.25
LN_EPS = 1e-5
LOG2E = 1.4426950408889634
FFN_RES = 0.5
GLA_HEADS, GLA_DK, GLA_DV, GLA_LR = 4, 64, 128, 16
GLA_GATE_NORM = 16.0
SB_HEADS, SB_DH = 8, 64
RW_HEADS, RW_N = 8, 64
RW_LR_W, RW_LR_A, RW_LR_G = 64, 64, 128
RW_LN_EPS = 64e-5
N_BRANCH = 3
MEM_HEADS = 4
PAGE_SIZE = 128

GLA_QK = GLA_HEADS * GLA_DK
GLA_V = GLA_HEADS * GLA_DV
SB_W = SB_HEADS * SB_DH
RW_W = RW_HEADS * RW_N
RW_LR = RW_LR_W + RW_LR_A + RW_LR_G
RW_COLS = 3 * RW_W + RW_LR

COL_GLA_Q, COL_GLA_K, COL_GLA_V, COL_GLA_G = 0, 256, 512, 1024
COL_SB_Q, COL_SB_K, COL_SB_V = 1536, 2048, 2560
COL_GATE = 3072
COL_RW = 6144
COL_RW_LR = COL_RW + 3 * RW_W
COL_GDA = COL_RW + RW_COLS
P_COLS = 8192

LANES = 128
CHUNK = 64
VMEM_LIMIT = 56 * 1024 * 1024


def _cparams(sem):
    return pltpu.CompilerParams(dimension_semantics=sem, vmem_limit_bytes=VMEM_LIMIT)


def _iota(shape, dim):
    return lax.broadcasted_iota(jnp.int32, shape, dim)


_DN = {"nn": (((1,), (0,)), ((), ())), "nt": (((1,), (1,)), ((), ())), "tn": (((0,), (0,)), ((), ()))}


def _pdot(a, b, kind="nn"):
    return lax.dot_general(a.astype(BF16), b.astype(BF16), _DN[kind], preferred_element_type=F32)


def _split3(x):
    h = x.astype(BF16)
    r = x - h.astype(F32)
    m = r.astype(BF16)
    lo = (r - m.astype(F32)).astype(BF16)
    return h, m, lo


def _exact_left(mat_bf16, x):
    h, m, lo = _split3(x)
    d = lambda y: jnp.dot(mat_bf16, y, preferred_element_type=F32)
    return d(h) + (d(m) + d(lo))


def _exact_right(x, mat_bf16, terms=3):
    h, m, lo = _split3(x)
    d = lambda y: jnp.dot(y, mat_bf16, preferred_element_type=F32)
    if terms == 2:
        return d(h) + d(m)
    return d(h) + (d(m) + d(lo))


def _seg_sum(x, seg_bf16):
    w = seg_bf16.shape[0]
    parts = [_exact_right(x[:, j * w:(j + 1) * w], seg_bf16, terms=2) for j in range(x.shape[1] // w)]
    return jnp.concatenate(parts, axis=1)


def _layer_norm(y, g, b):
    mu = jnp.mean(y, axis=-1, keepdims=True)
    d = y - mu
    var = jnp.mean(d * d, axis=-1, keepdims=True)
    return d * lax.rsqrt(var + LN_EPS) * g + b


def _log_sigmoid(x):
    return jnp.minimum(x, 0.0) - jnp.log1p(jnp.exp(-jnp.abs(x)))


def _ffn_ln_body(x_ref, w1_ref, w3_ref, w2_ref, g_ref, b_ref, o_ref, xb_ref, acc_ref):
    j = pl.program_id(1)

    @pl.when(j == 0)
    def _():
        xb_ref[...] = x_ref[...].astype(BF16)
        acc_ref[...] = jnp.zeros_like(acc_ref)

    xb = xb_ref[...]
    h1 = jnp.dot(xb, w1_ref[...], preferred_element_type=F32)
    h3 = jnp.dot(xb, w3_ref[...], preferred_element_type=F32)
    h = (h1 * jax.nn.sigmoid(h1)) * h3
    acc_ref[...] += jnp.dot(h.astype(BF16), w2_ref[...], preferred_element_type=F32)

    @pl.when(j == pl.num_programs(1) - 1)
    def _():
        y = DN_ALPHA * x_ref[...] + FFN_RES * acc_ref[...]
        o_ref[...] = _layer_norm(y, g_ref[...], b_ref[...])


def _ffn_ln(x, w1, w3, w2, g, b):
    n, d = x.shape
    dff = w1.shape[1]
    tm = min(n, 512)
    tf = dff // 2 if (dff // 2) % LANES == 0 else 256
    assert n % tm == 0 and dff % tf == 0
    return pl.pallas_call(
        _ffn_ln_body,
        grid=(n // tm, dff // tf),
        in_specs=[
            pl.BlockSpec((tm, d), lambda i, j: (i, 0)),
            pl.BlockSpec((d, tf), lambda i, j: (0, j)),
            pl.BlockSpec((d, tf), lambda i, j: (0, j)),
            pl.BlockSpec((tf, d), lambda i, j: (j, 0)),
            pl.BlockSpec((1, d), lambda i, j: (0, 0)),
            pl.BlockSpec((1, d), lambda i, j: (0, 0)),
        ],
        out_specs=pl.BlockSpec((tm, d), lambda i, j: (i, 0)),
        out_shape=jax.ShapeDtypeStruct((n, d), F32),
        scratch_shapes=[pltpu.VMEM((tm, d), BF16), pltpu.VMEM((tm, d), F32)],
        compiler_params=_cparams(("parallel", "arbitrary")),
        name="ffn_ln",
    )(x, w1, w3, w2, g, b)


def _mm_body(x_ref, w_ref, o_ref, xb_ref):
    @pl.when(pl.program_id(1) == 0)
    def _():
        xb_ref[...] = x_ref[...].astype(BF16)

    o_ref[...] = jnp.dot(xb_ref[...], w_ref[...], preferred_element_type=F32)


def _matmul(x, w):
    n, k = x.shape
    m = w.shape[1]
    tm = min(n, 1024)
    tn = min(m, 1024)
    assert n % tm == 0 and m % tn == 0
    return pl.pallas_call(
        _mm_body,
        grid=(n // tm, m // tn),
        in_specs=[
            pl.BlockSpec((tm, k), lambda i, j: (i, 0)),
            pl.BlockSpec((k, tn), lambda i, j: (0, j)),
        ],
        out_specs=pl.BlockSpec((tm, tn), lambda i, j: (i, j)),
        out_shape=jax.ShapeDtypeStruct((n, m), F32),
        scratch_shapes=[pltpu.VMEM((tm, k), BF16)],
        compiler_params=_cparams(("parallel", "arbitrary")),
        name="proj_matmul",
    )(x, w)


def _merge_body(x_ref, oa_ref, ob_ref, oc_ref, pg0_ref, pg1_ref, pg2_ref, bg_ref, wbr_ref, wo_ref,
                g_ref, b_ref, o_ref):
    s = None
    for n, (o_r, pg_r) in enumerate(((oa_ref, pg0_ref), (ob_ref, pg1_ref), (oc_ref, pg2_ref))):
        br = jnp.dot(o_r[...].astype(BF16), wbr_ref[n], preferred_element_type=F32)
        term = jax.nn.sigmoid(pg_r[...] + bg_ref[n]) * br
        s = term if s is None else s + term
    mix = jnp.dot(s.astype(BF16), wo_ref[...], preferred_element_type=F32)
    o_ref[...] = _layer_norm(DN_ALPHA * x_ref[...] + mix, g_ref[...], b_ref[...])


def _merge_ln(x, oa, ob, oc, proj, b_gate, w_br, w_o, g, b):
    n, d = x.shape
    bw = oa.shape[1]
    tm = min(n, 512)
    gate_blk = COL_GATE // d
    row = lambda i: (i, 0)
    return pl.pallas_call(
        _merge_body,
        grid=(n // tm,),
        in_specs=[
            pl.BlockSpec((tm, d), row),
            pl.BlockSpec((tm, bw), row),
            pl.BlockSpec((tm, bw), row),
            pl.BlockSpec((tm, bw), row),
            pl.BlockSpec((tm, d), lambda i: (i, gate_blk)),
            pl.BlockSpec((tm, d), lambda i: (i, gate_blk + 1)),
            pl.BlockSpec((tm, d), lambda i: (i, gate_blk + 2)),
            pl.BlockSpec((N_BRANCH, 1, d), lambda i: (0, 0, 0)),
            pl.BlockSpec((N_BRANCH, bw, d), lambda i: (0, 0, 0)),
            pl.BlockSpec((d, d), lambda i: (0, 0)),
            pl.BlockSpec((1, d), lambda i: (0, 0)),
            pl.BlockSpec((1, d), lambda i: (0, 0)),
        ],
        out_specs=pl.BlockSpec((tm, d), row),
        out_shape=jax.ShapeDtypeStruct((n, d), F32),
        compiler_params=_cparams(("parallel",)),
        name="merge_ln",
    )(x, oa, ob, oc, proj, proj, proj, b_gate, w_br, w_o, g, b)


def _memattn_body(x_ref, wq_ref, wo_ref, mk_ref, mv_ref, g_ref, b_ref, o_ref, q_ref, acc_ref, *,
                  t_seq, tiles_per_batch, masked):
    i = pl.program_id(0)
    j = pl.program_id(1)
    tm, d = x_ref.shape
    dh = d // MEM_HEADS

    @pl.when(j == 0)
    def _():
        q_ref[...] = jnp.dot(x_ref[...].astype(BF16), wq_ref[...], preferred_element_type=F32).astype(BF16)
        acc_ref[...] = jnp.zeros_like(acc_ref)

    if masked:
        row = i * tm + _iota((tm, 1), 0)
        mem_b = i // tiles_per_batch + j
        keep = (row >= mem_b * t_seq) & (row < (mem_b + 1) * t_seq)
    for h in range(MEM_HEADS):
        sl = slice(h * dh, (h + 1) * dh)
        qh = q_ref[:, sl]
        kh = mk_ref[0, :, sl].astype(BF16)
        vh = mv_ref[0, :, sl].astype(BF16)
        s = lax.dot_general(qh, kh, _DN["nt"], preferred_element_type=F32) * (dh ** -0.5)
        e = jnp.exp(s - jnp.max(s, axis=-1, keepdims=True))
        att = e / jnp.sum(e, axis=-1, keepdims=True)
        oh = jnp.dot(att.astype(BF16), vh, preferred_element_type=F32)
        if masked:
            oh = jnp.where(keep, oh, 0.0)
        acc_ref[:, sl] += oh

    @pl.when(j == pl.num_programs(1) - 1)
    def _():
        xm = jnp.dot(acc_ref[...].astype(BF16), wo_ref[...], preferred_element_type=F32)
        o_ref[...] = _layer_norm(DN_ALPHA * x_ref[...] + xm, g_ref[...], b_ref[...])


def _memattn_ln(x, wq, wo, mem_k, mem_v, g, b, t_seq):
    n, d = x.shape
    nb, m, _ = mem_k.shape
    tm = min(n, 512)
    if t_seq >= tm:
        assert t_seq % tm == 0
        tiles_per_batch, nj, masked = t_seq // tm, 1, False
    else:
        assert tm % t_seq == 0 and n == tm
        tiles_per_batch, nj, masked = 1, nb, True
    body = functools.partial(_memattn_body, t_seq=t_seq, tiles_per_batch=tiles_per_batch, masked=masked)
    mem_map = lambda i, j: (i // tiles_per_batch + j, 0, 0)
    return pl.pallas_call(
        body,
        grid=(n // tm, nj),
        in_specs=[
            pl.BlockSpec((tm, d), lambda i, j: (i, 0)),
            pl.BlockSpec((d, d), lambda i, j: (0, 0)),
            pl.BlockSpec((d, d), lambda i, j: (0, 0)),
            pl.BlockSpec((1, m, d), mem_map),
            pl.BlockSpec((1, m, d), mem_map),
            pl.BlockSpec((1, d), lambda i, j: (0, 0)),
            pl.BlockSpec((1, d), lambda i, j: (0, 0)),
        ],
        out_specs=pl.BlockSpec((tm, d), lambda i, j: (i, 0)),
        out_shape=jax.ShapeDtypeStruct((n, d), F32),
        scratch_shapes=[pltpu.VMEM((tm, d), BF16), pltpu.VMEM((tm, d), F32)],
        compiler_params=_cparams(("parallel", "arbitrary")),
        name="memattn_ln",
    )(x, wq, wo, mem_k, mem_v, g, b)


def _stack2(x):
    return jnp.concatenate([x, x], axis=0)


def _head_rows_mask(c, width, seg):
    r = _iota((2 * c, width), 0)
    l = _iota((2 * c, width), 1)
    return ((r < c) & (l < seg)) | ((r >= c) & (l >= seg))


GLA_CHUNKS_PER_ITER = 4


def _gla_body(q_ref, k_ref, v_ref, g_ref, gda_ref, wg2_ref, bg_ref, ng_ref, s0_ref, o_ref, sout_ref, s_ref, *,
              t_valid):
    t = pl.program_id(1)
    tt = q_ref.shape[0]
    c = CHUNK
    npair = GLA_HEADS // 2
    kw, vw = 2 * GLA_DK, 2 * GLA_DV

    @pl.when(t == 0)
    def _():
        s_ref[...] = s0_ref[0]

    ri = _iota((c, c), 0)
    ci = _iota((c, c), 1)
    tri_incl = (ci <= ri).astype(F32).astype(BF16)
    r2 = _iota((2 * c, 2 * c), 0)
    c2 = _iota((2 * c, 2 * c), 1)
    same_head = ((r2 < c) & (c2 < c)) | ((r2 >= c) & (c2 >= c))
    att_mask = same_head & ((c2 & (c - 1)) <= (r2 & (c - 1)))
    qmask = _head_rows_mask(c, kw, GLA_DK)
    omask = _head_rows_mask(c, vw, GLA_DV)
    sr = _iota((vw, kw), 0)
    sc = _iota((vw, kw), 1)
    bd_mask = ((sr < GLA_DV) & (sc < GLA_DK)) | ((sr >= GLA_DV) & (sc >= GLA_DK))

    cpi = min(GLA_CHUNKS_PER_ITER, tt // c)
    pairs = range(npair)
    kls = [slice(p * kw, (p + 1) * kw) for p in pairs]
    vls = [slice(p * vw, (p + 1) * vw) for p in pairs]

    def chunk_group(ig, carry):
        units = [(j, p) for j in range(cpi) for p in pairs]
        grows = pl.ds(pl.multiple_of(ig * (cpi * c), cpi * c), cpi * c)
        rows = [pl.ds(pl.multiple_of((ig * cpi + j) * c, c), c) for j in range(cpi)]
        la = _log_sigmoid(_pdot(gda_ref[grows, :], wg2_ref[...]) + bg_ref[...]) / GLA_GATE_NORM
        kk = k_ref[grows, :]
        if t_valid is not None:
            valid = (t * tt + ig * (cpi * c) + _iota((cpi * c, 1), 0)) < t_valid
            la = jnp.where(valid, la, 0.0)
            kk = jnp.where(valid, kk, 0.0)
        la3 = _split3(la)
        bcs = [sum(jnp.dot(tri_incl, part[j * c:(j + 1) * c], preferred_element_type=F32) for part in la3)
               for j in range(cpi)]
        b_last = [bc[c - 1:c, :] for bc in bcs]
        e_last = [jnp.exp(b) for b in b_last]
        qs = q_ref[grows, :] * (GLA_DK ** -0.5)
        q_dec = [qs[j * c:(j + 1) * c] * jnp.exp(bcs[j]) for j in range(cpi)]
        k_inv = [kk[j * c:(j + 1) * c] * jnp.exp(-bcs[j]) for j in range(cpi)]
        k_end = [kk[j * c:(j + 1) * c] * jnp.exp(b_last[j] - bcs[j]) for j in range(cpi)]
        vv = [v_ref[rows[j], :] for j in range(cpi)]
        qd2 = {(j, p): jnp.where(qmask, _stack2(q_dec[j][:, kls[p]]), 0.0).astype(BF16) for j, p in units}
        att = {(j, p): jnp.where(att_mask, _pdot(qd2[j, p], _stack2(k_inv[j][:, kls[p]]), "nt"), 0.0)
               for j, p in units}
        intra = {(j, p): jnp.where(omask, _pdot(att[j, p], _stack2(vv[j][:, vls[p]])), 0.0) for j, p in units}
        s_cur = [s_ref[p] for p in pairs]
        for j in range(cpi):
            o2 = [_pdot(qd2[j, p], s_cur[p], "nt") + intra[j, p] for p in pairs]
            upd = [_pdot(vv[j][:, vls[p]], k_end[j][:, kls[p]], "tn") for p in pairs]
            s_cur = [s_cur[p] * e_last[j][:, kls[p]] + jnp.where(bd_mask, upd[p], 0.0) for p in pairs]
            normed = []
            for p in pairs:
                op = o2[p][:c] + o2[p][c:]
                for h in range(2):
                    oh = op[:, h * GLA_DV:(h + 1) * GLA_DV]
                    normed.append(oh * lax.rsqrt(jnp.mean(oh * oh, axis=-1, keepdims=True) + LN_EPS))
            o = jnp.concatenate(normed, axis=1) * ng_ref[...]
            gg = g_ref[rows[j], :]
            o_ref[rows[j], :] = o * (gg * jax.nn.sigmoid(gg))
        for p in pairs:
            s_ref[p] = s_cur[p]
        return carry

    lax.fori_loop(0, tt // (c * cpi), chunk_group, 0)

    @pl.when(t == pl.num_programs(1) - 1)
    def _():
        sout_ref[0] = s_ref[...]


def _gla(proj, nb, t_len, wg2p, bg, ng, s0bd, t_valid):
    tt = min(t_len, 512)
    nt = t_len // tt
    npair = GLA_HEADS // 2
    kw, vw = 2 * GLA_DK, 2 * GLA_DV
    body = functools.partial(_gla_body, t_valid=t_valid)

    def col(width, off):
        blk = off // width
        return pl.BlockSpec((tt, width), lambda b, t: (b * nt + t, blk))

    const2 = lambda b, t: (0, 0)
    return pl.pallas_call(
        body,
        grid=(nb, nt),
        in_specs=[
            col(GLA_QK, COL_GLA_Q), col(GLA_QK, COL_GLA_K), col(GLA_V, COL_GLA_V), col(GLA_V, COL_GLA_G),
            col(LANES, COL_GDA),
            pl.BlockSpec((LANES, GLA_QK), const2),
            pl.BlockSpec((1, GLA_QK), const2),
            pl.BlockSpec((1, GLA_V), const2),
            pl.BlockSpec((1, npair, vw, kw), lambda b, t: (b, 0, 0, 0)),
        ],
        out_specs=[
            pl.BlockSpec((tt, GLA_V), lambda b, t: (b * nt + t, 0)),
            pl.BlockSpec((1, npair, vw, kw), lambda b, t: (b, 0, 0, 0)),
        ],
        out_shape=[
            jax.ShapeDtypeStruct((nb * t_len, GLA_V), F32),
            jax.ShapeDtypeStruct((nb, npair, vw, kw), F32),
        ],
        scratch_shapes=[pltpu.VMEM((npair, vw, kw), F32)],
        compiler_params=_cparams(("parallel", "arbitrary")),
        name="gla",
    )(proj, proj, proj, proj, proj, wg2p, bg, ng, s0bd)


RW_CHUNKS_PER_ITER = 4


def _rwkv_body(r_ref, k_ref, v_ref, lr_ref, shr_ref, shk_ref, shv_ref, shlr_ref, mur_ref, muk_ref, muv_ref,
               mulr_ref, w0_ref, w2_ref, a0_ref, a2_ref, g2_ref, kkp_ref, ka_ref, rk_ref, lng_ref, lnb_ref,
               seg_ref, s0_ref, o_ref, sout_ref,
               s_ref, cr_ref, ck_ref, cv_ref, clr_ref, rs_ref, ws_ref, ks_ref, vs_ref, as_ref, bs_ref, ys_ref, *,
               t_valid):
    t = pl.program_id(1)
    tt = r_ref.shape[0]
    c = CHUNK
    npair = RW_HEADS // 2
    pw = 2 * RW_N

    @pl.when(t == 0)
    def _():
        s_ref[...] = s0_ref[0]
        cr_ref[...] = shr_ref[0]
        ck_ref[...] = shk_ref[0]
        cv_ref[...] = shv_ref[0]
        clr_ref[...] = shlr_ref[0]

    def lerp(x_ref, carry_ref, mu_ref):
        x = x_ref[...]
        prev = pltpu.roll(x, 1, 0)
        prev = jnp.where(_iota(x.shape, 0) == 0, carry_ref[...], prev)
        carry_ref[...] = x_ref[pl.ds(tt - 1, 1), :]
        return x + (prev - x) * mu_ref[...]

    xr = lerp(r_ref, cr_ref, mur_ref)
    xk = lerp(k_ref, ck_ref, muk_ref)
    xv = lerp(v_ref, cv_ref, muv_ref)
    xlr = lerp(lr_ref, clr_ref, mulr_ref)
    w_log = _log_sigmoid(w0_ref[...] + _pdot(jnp.tanh(xlr), w2_ref[...])) - 0.5
    wdec = -jnp.exp(w_log)
    a = jax.nn.sigmoid(a0_ref[...] + _pdot(xlr, a2_ref[...]))
    g = _pdot(jax.nn.sigmoid(xlr), g2_ref[...])
    kk = xk * kkp_ref[...]
    kkn = kk / jnp.maximum(jnp.sqrt(_seg_sum(kk * kk, seg_ref[...])), 1e-12)
    kmod = xk * (1.0 + (a - 1.0) * ka_ref[...])
    av = -kkn
    bv = kkn * a
    if t_valid is not None:
        valid = (t * tt + _iota((tt, 1), 0)) < t_valid
        wdec = jnp.where(valid, wdec, 0.0)
        av = jnp.where(valid, av, 0.0)
        bv = jnp.where(valid, bv, 0.0)
        kmod = jnp.where(valid, kmod, 0.0)
        xv = jnp.where(valid, xv, 0.0)
    rs_ref[...] = xr
    ws_ref[...] = wdec
    ks_ref[...] = kmod
    vs_ref[...] = xv
    as_ref[...] = av
    bs_ref[...] = bv

    ri = _iota((c, c), 0)
    ci = _iota((c, c), 1)
    tri_incl = (ci <= ri).astype(F32).astype(BF16)
    r2 = _iota((2 * c, 2 * c), 0)
    c2 = _iota((2 * c, 2 * c), 1)
    same_head = ((r2 < c) & (c2 < c)) | ((r2 >= c) & (c2 >= c))
    strict = same_head & ((c2 & (c - 1)) < (r2 & (c - 1)))
    incl = same_head & ((c2 & (c - 1)) <= (r2 & (c - 1)))
    eye = (r2 == c2).astype(F32)
    hmask = _head_rows_mask(c, pw, RW_N)
    bd_mask = _head_rows_mask(RW_N, pw, RW_N)

    cpi = min(RW_CHUNKS_PER_ITER, tt // c)
    pairs = range(npair)
    lanes = [slice(p * pw, (p + 1) * pw) for p in pairs]
    units = [(j, p) for j in range(cpi) for p in pairs]

    def chunk_group(ig, carry):
        rows, b_e, k_e, vv, e_last = [], [], [], [], []
        ar4, bk4, v2 = {}, {}, {}
        for j in range(cpi):
            rows.append(pl.ds(pl.multiple_of((ig * cpi + j) * c, c), c))
            w = ws_ref[rows[j], :]
            cs = _exact_left(tri_incl, w)
            c_last = cs[c - 1:c, :]
            e_neg = jnp.exp(-cs)
            e_end = jnp.exp(c_last - cs)
            e_last.append(jnp.exp(c_last))
            a_t = as_ref[rows[j], :] * jnp.exp(cs - w)
            r_t = rs_ref[rows[j], :] * jnp.exp(cs)
            bb = bs_ref[rows[j], :]
            kc = ks_ref[rows[j], :]
            b_t = bb * e_neg
            k_t = kc * e_neg
            b_e.append(bb * e_end)
            k_e.append(kc * e_end)
            vv.append(vs_ref[rows[j], :])
            for p in pairs:
                ln = lanes[p]
                ar4[j, p] = jnp.concatenate([jnp.where(hmask, _stack2(a_t[:, ln]), 0.0),
                                             jnp.where(hmask, _stack2(r_t[:, ln]), 0.0)], axis=0).astype(BF16)
                bk4[j, p] = jnp.concatenate([_stack2(b_t[:, ln]), _stack2(k_t[:, ln])], axis=0).astype(BF16)
                v2[j, p] = _stack2(vv[j][:, ln]).astype(BF16)
        g4 = {u: _pdot(ar4[u], bk4[u], "nt") for u in units}
        n_ab = {u: jnp.where(strict, g4[u][:2 * c, :2 * c], 0.0) for u in units}
        n_ak = {u: jnp.where(strict, g4[u][:2 * c, 2 * c:], 0.0) for u in units}
        n_r = {u: jnp.concatenate([jnp.where(incl, g4[u][2 * c:, :2 * c], 0.0),
                                   jnp.where(incl, g4[u][2 * c:, 2 * c:], 0.0)], axis=1).astype(BF16)
               for u in units}
        nakv = {u: jnp.where(hmask, _pdot(n_ak[u], v2[u]), 0.0) for u in units}
        tinv = {u: eye + n_ab[u] for u in units}
        x = n_ab
        for _ in range(int(math.log2(c)) - 1):
            x = {u: _pdot(x[u], x[u]) for u in units}
            tinv = {u: tinv[u] + _pdot(tinv[u], x[u]) for u in units}
        s_cur = [s_ref[p] for p in pairs]
        for j in range(cpi):
            as4 = [_pdot(ar4[j, p], s_cur[p], "nt") for p in pairs]
            u2 = [_pdot(tinv[j, p], as4[p][:2 * c] + nakv[j, p]) for p in pairs]
            y2 = [as4[p][2 * c:] + jnp.where(
                hmask, _pdot(n_r[j, p], jnp.concatenate([u2[p].astype(BF16), v2[j, p]], axis=0)), 0.0)
                  for p in pairs]
            upd = [_pdot(jnp.concatenate([u2[p][:c] + u2[p][c:], vv[j][:, lanes[p]]], axis=0),
                         jnp.concatenate([b_e[j][:, lanes[p]], k_e[j][:, lanes[p]]], axis=0), "tn") for p in pairs]
            s_cur = [s_cur[p] * e_last[j][:, lanes[p]] + jnp.where(bd_mask, upd[p], 0.0) for p in pairs]
            ys_ref[rows[j], :] = jnp.concatenate([y2[p][:c] + y2[p][c:] for p in pairs], axis=1)
        for p in pairs:
            s_ref[p] = s_cur[p]
        return carry

    lax.fori_loop(0, tt // (c * cpi), chunk_group, 0)

    y = ys_ref[...]
    seg = seg_ref[...]
    inv_n = 1.0 / RW_N
    mu = _seg_sum(y, seg) * inv_n
    d = y - mu
    var = _seg_sum(d * d, seg) * inv_n
    yn = d * lax.rsqrt(var + RW_LN_EPS) * lng_ref[...] + lnb_ref[...]
    bonus = _seg_sum(xr * kmod * rk_ref[...], seg)
    o_ref[...] = (yn + bonus * xv) * g

    @pl.when(t == pl.num_programs(1) - 1)
    def _():
        sout_ref[0] = s_ref[...]


def _rwkv(proj, nb, t_len, shifts, prm, seg, s0bd, t_valid):
    tt = min(t_len, 256)
    nt = t_len // tt
    npair = RW_HEADS // 2
    pw = 2 * RW_N
    body = functools.partial(_rwkv_body, t_valid=t_valid)

    def col(width, off):
        blk = off // width
        return pl.BlockSpec((tt, width), lambda b, t: (b * nt + t, blk))

    def per_batch(width):
        return pl.BlockSpec((1, 1, width), lambda b, t: (b, 0, 0))

    def const(shape):
        return pl.BlockSpec(shape, lambda b, t: (0,) * len(shape))

    vec = const((1, RW_W))
    lrm = const((RW_LR, RW_W))
    state = pl.BlockSpec((1, npair, pw, pw), lambda b, t: (b, 0, 0, 0))
    tile = lambda: pltpu.VMEM((tt, RW_W), F32)
    return pl.pallas_call(
        body,
        grid=(nb, nt),
        in_specs=[
            col(RW_W, COL_RW), col(RW_W, COL_RW + RW_W), col(RW_W, COL_RW + 2 * RW_W), col(RW_LR, COL_RW_LR),
            per_batch(RW_W), per_batch(RW_W), per_batch(RW_W), per_batch(RW_LR),
            vec, vec, vec, const((1, RW_LR)),
            vec, lrm, vec, lrm, lrm, vec, vec, vec, vec, vec,
            const(seg.shape),
            state,
        ],
        out_specs=[pl.BlockSpec((tt, RW_W), lambda b, t: (b * nt + t, 0)), state],
        out_shape=[
            jax.ShapeDtypeStruct((nb * t_len, RW_W), F32),
            jax.ShapeDtypeStruct((nb, npair, pw, pw), F32),
        ],
        scratch_shapes=[
            pltpu.VMEM((npair, pw, pw), F32),
            pltpu.VMEM((1, RW_W), F32), pltpu.VMEM((1, RW_W), F32), pltpu.VMEM((1, RW_W), F32),
            pltpu.VMEM((1, RW_LR), F32),
            tile(), tile(), tile(), tile(), tile(), tile(), tile(),
        ],
        compiler_params=_cparams(("parallel", "arbitrary")),
        name="rwkv7",
    )(proj, proj, proj, proj, *shifts, *prm, seg, s0bd)


SB_TQ = 512
SB_SUB = 256


def _sbp_body(q_ref, k_ref, v_ref, bias_ref, ms_ref, o_ref):
    i = pl.program_id(2)
    tq = q_ref.shape[0]
    tk = tq
    sub = min(SB_SUB, tk)
    nsub = tk // sub
    ms = ms_ref[...]
    lane = _iota((1, LANES), 1)
    q = q_ref[...] * (SB_DH ** -0.5 * LOG2E)
    causal = _iota((tq, tk), 1) < _iota((tq, tk), 0)
    out = jnp.zeros((tq, LANES), F32)
    for h in range(2):
        hm = (lane >= h * SB_DH) & (lane < (h + 1) * SB_DH)
        qh = jnp.where(hm, q, 0.0).astype(BF16)
        bias = bias_ref[0, h:h + 1, :] * LOG2E

        def block(kb, carry, masked, qh=qh, bias=bias):
            acc, later = carry
            rows = pl.ds(pl.multiple_of(kb * tk, tk), tk)
            kblk = k_ref[rows, :].astype(BF16)
            vblk = v_ref[rows, :].astype(BF16)
            z = lax.dot_general(qh, kblk, _DN["nt"], preferred_element_type=F32) + bias
            sp = jnp.maximum(z, 0.0) + jnp.log2(1.0 + jnp.exp2(-jnp.abs(z)))
            if masked:
                sp = jnp.where(causal, sp, 0.0)
            sls = [slice(j * sub, (j + 1) * sub) for j in range(nsub)]
            ws = [jnp.exp2(z[:, sl] + jnp.dot(sp[:, sl].astype(BF16), ms, preferred_element_type=F32)) for sl in sls]
            if masked:
                ws = [jnp.where(causal[:, sl], w, 0.0) for sl, w in zip(sls, ws)]
            pvs = [jnp.dot(w.astype(BF16), vblk[sl], preferred_element_type=F32) for sl, w in zip(sls, ws)]
            tots = [jnp.sum(sp[:, sl], axis=1, keepdims=True) for sl in sls]
            for j in reversed(range(nsub)):
                acc = acc + jnp.exp2(later) * pvs[j]
                later = later - tots[j]
            return acc, later

        carry = block(i, (jnp.zeros((tq, LANES), F32), jnp.zeros((tq, 1), F32)), True)
        acc, _ = lax.fori_loop(0, i, lambda n, cr, blk=block: blk(i - 1 - n, cr, False), carry)
        out = out + jnp.where(hm, acc, 0.0)
    o_ref[...] = out


def _sb_prompt(proj, nb, t_len, bias_rows, ms):
    tq = min(SB_TQ, t_len)
    nq = t_len // tq
    npair = SB_HEADS // 2
    qb, kb, vb = COL_SB_Q // LANES, COL_SB_K // LANES, COL_SB_V // LANES
    return pl.pallas_call(
        _sbp_body,
        grid=(nb, npair, nq),
        in_specs=[
            pl.BlockSpec((tq, LANES), lambda b, p, i: (b * nq + i, qb + p)),
            pl.BlockSpec((t_len, LANES), lambda b, p, i: (b, kb + p)),
            pl.BlockSpec((t_len, LANES), lambda b, p, i: (b, vb + p)),
            pl.BlockSpec((1, 2, tq), lambda b, p, i: (p, 0, 0)),
            pl.BlockSpec(ms.shape, lambda b, p, i: (0, 0)),
        ],
        out_specs=pl.BlockSpec((tq, LANES), lambda b, p, i: (b * nq + i, p)),
        out_shape=jax.ShapeDtypeStruct((nb * t_len, SB_W), F32),
        compiler_params=_cparams(("parallel", "parallel", "arbitrary")),
        name="sb_prompt",
    )(proj, proj, proj, bias_rows, ms)


SB_PAGES_PER_STEP = 16


def _sbs_body(pt_ref, q_ref, bias_ref, knew_ref, vnew_ref, *rest):
    g_pages = SB_PAGES_PER_STEP
    k_refs = rest[:g_pages]
    v_refs = rest[g_pages:2 * g_pages]
    ms_ref, o_ref, acc_ref, later_ref = rest[2 * g_pages:]
    s = pl.program_id(1)
    nrow, width = q_ref.shape[1:]
    nkey = knew_ref.shape[2]
    tpad = nrow // SB_HEADS
    q2 = (q_ref[0] * (SB_DH ** -0.5)).astype(BF16)
    bias = bias_ref[...]
    ms = ms_ref[...]
    fresh_ok = _iota((nrow, nkey), 1) < (_iota((nrow, nkey), 0) & (tpad - 1))

    def page_terms(kt, keep):
        z = jnp.dot(q2, kt.astype(BF16), preferred_element_type=F32) + bias
        sp = jnp.maximum(z, 0.0) + jnp.log1p(jnp.exp(-jnp.abs(z)))
        zs = z - sp
        if keep is not None:
            sp = jnp.where(keep, sp, 0.0)
        return zs, _exact_right(sp, ms, terms=2), jnp.sum(sp, axis=1, keepdims=True)

    def combine(pages, keep, carry):
        acc, later = carry
        for (zs, cum, tot), vt in pages:
            w = jnp.exp(zs + (cum + later))
            later = later - tot
            if keep is not None:
                w = jnp.where(keep, w, 0.0)
            acc = acc + lax.dot_general(w.astype(BF16), vt.astype(BF16), _DN["nt"], preferred_element_type=F32)
        return acc, later

    @pl.when(s == 0)
    def _():
        zero = (jnp.zeros(acc_ref.shape, F32), jnp.zeros(later_ref.shape, F32))
        acc_ref[...], later_ref[...] = combine([(page_terms(knew_ref[0], fresh_ok), vnew_ref[0])], fresh_ok, zero)

    @pl.when(s > 0)
    def _():
        flat = lambda ref: ref[0, 0].reshape(width, nkey)
        pages = [(page_terms(flat(k_refs[g]), None), flat(v_refs[g])) for g in range(g_pages)]
        acc_ref[...], later_ref[...] = combine(pages, None, (acc_ref[...], later_ref[...]))

    @pl.when(s == pl.num_programs(1) - 1)
    def _():
        o_ref[0] = acc_ref[...]


def _sb_sample(qbd, bias_col, knew_t, vnew_t, cache_kt, cache_vt, page_table, layer, ms):
    nb, nrow, width = qbd.shape
    nkey = knew_t.shape[2]
    n_pages = page_table.shape[1]
    g_pages = SB_PAGES_PER_STEP
    assert n_pages % g_pages == 0
    nsteps = n_pages // g_pages

    def page_spec(g):
        def imap(b, s, pt):
            page = n_pages - 1 - (jnp.maximum(s, 1) - 1) * g_pages - g
            return (layer, pt[b, page], 0, 0, 0)
        return pl.BlockSpec((1, 1) + cache_kt.shape[2:], imap)

    per_b = lambda b, s, pt: (b, 0, 0)
    grid_spec = pltpu.PrefetchScalarGridSpec(
        num_scalar_prefetch=1,
        grid=(nb, nsteps + 1),
        in_specs=[
            pl.BlockSpec((1, nrow, width), per_b),
            pl.BlockSpec((nrow, 1), lambda b, s, pt: (0, 0)),
            pl.BlockSpec((1, width, nkey), per_b),
            pl.BlockSpec((1, width, nkey), per_b),
            *[page_spec(g) for g in range(g_pages)],
            *[page_spec(g) for g in range(g_pages)],
            pl.BlockSpec(ms.shape, lambda b, s, pt: (0, 0)),
        ],
        out_specs=pl.BlockSpec((1, nrow, width), per_b),
        scratch_shapes=[pltpu.VMEM((nrow, width), F32), pltpu.VMEM((nrow, 1), F32)],
    )
    return pl.pallas_call(
        _sbs_body,
        grid_spec=grid_spec,
        out_shape=jax.ShapeDtypeStruct((nb, nrow, width), F32),
        compiler_params=_cparams(("parallel", "arbitrary")),
        name="sb_sample",
    )(page_table, qbd, bias_col, knew_t, vnew_t, *([cache_kt] * g_pages), *([cache_vt] * g_pages), ms)


def _blockdiag_in(s, pairs):
    b, h, do, di = s.shape
    s = s.reshape(b, pairs, 2, do, di)
    return jnp.einsum("bphvk,hg->bphvgk", s, jnp.eye(2, dtype=s.dtype)).reshape(b, pairs, 2 * do, 2 * di)


def _blockdiag_out(sbd, do, di):
    b, pairs = sbd.shape[:2]
    x = sbd.reshape(b, pairs, 2, do, 2, di)
    return jnp.einsum("bphvhk->bphvk", x).reshape(b, 2 * pairs, do, di)


def _pack_layer(l, ln_g, ln_b, ffn_w1, ffn_w3, ffn_w2, w_in, b_gate, gla_wg2, gla_bg, gla_norm_g, sb_bias, rw_mu,
                rw_w0, rw_w2, rw_a0, rw_a2, rw_g2, rw_kk, rw_ka, rw_rk, rw_lnx_g, rw_lnx_b, w_br, w_o, mem_wq,
                mem_wk, mem_wv, mem_wo):
    d = w_in.shape[1]
    gla_cols = 2 * GLA_QK + 2 * GLA_V + GLA_LR
    o_sb = gla_cols
    o_rw = o_sb + 3 * SB_W
    o_gate = o_rw + RW_COLS
    w = w_in[l]
    w_re = jnp.concatenate([
        w[:, :gla_cols - GLA_LR], w[:, o_sb:o_rw], w[:, o_gate:], w[:, o_rw:o_gate],
        w[:, gla_cols - GLA_LR:gla_cols], jnp.zeros((d, P_COLS - COL_GDA - GLA_LR), F32)], axis=1)
    assert w_re.shape[1] == P_COLS
    row = lambda v: v.reshape(1, -1)
    mu = rw_mu[l]

    def lr_pad(m, off):
        return jnp.zeros((RW_LR, RW_W), F32).at[off:off + m.shape[0]].set(m).astype(BF16)

    return dict(
        ln_g=[row(ln_g[l, i]) for i in range(4)], ln_b=[row(ln_b[l, i]) for i in range(4)],
        ffn=[(ffn_w1[l, i].astype(BF16), ffn_w3[l, i].astype(BF16), ffn_w2[l, i].astype(BF16)) for i in range(2)],
        w_in=w_re.astype(BF16),
        b_gate=b_gate[l].reshape(N_BRANCH, 1, -1), w_br=w_br[l].astype(BF16), w_o=w_o[l].astype(BF16),
        gla_wg2=jnp.zeros((LANES, GLA_QK), F32).at[:GLA_LR].set(gla_wg2[l]).astype(BF16),
        gla_bg=row(gla_bg[l]), gla_ng=row(jnp.tile(gla_norm_g[l], GLA_HEADS)),
        sb_bias=sb_bias[l],
        rw=[row(mu[:RW_W]), row(mu[RW_W:2 * RW_W]), row(mu[2 * RW_W:3 * RW_W]), row(mu[3 * RW_W:]),
            row(rw_w0[l]), lr_pad(rw_w2[l], 0), row(rw_a0[l]), lr_pad(rw_a2[l], RW_LR_W),
            lr_pad(rw_g2[l], RW_LR_W + RW_LR_A), row(rw_kk[l]), row(rw_ka[l]), row(rw_rk[l].reshape(-1)),
            row(rw_lnx_g[l]), row(rw_lnx_b[l])],
        mem_wq=mem_wq[l].astype(BF16), mem_wo=mem_wo[l].astype(BF16),
        mem_wkv=jnp.concatenate([mem_wk[l], mem_wv[l]], axis=1).astype(BF16),
    )


def _split_shift(sh):
    return [sh[:, None, :RW_W], sh[:, None, RW_W:2 * RW_W], sh[:, None, 2 * RW_W:3 * RW_W], sh[:, None, 3 * RW_W:]]


def _mixers_recurrent(proj, nb, t_len, pk, seg, gla_s0, rw_s0, rw_shift0, t_valid):
    oa, gla_s = _gla(proj, nb, t_len, pk["gla_wg2"], pk["gla_bg"], pk["gla_ng"],
                     _blockdiag_in(jnp.swapaxes(gla_s0, 2, 3), GLA_HEADS // 2), t_valid)
    oc, rw_s = _rwkv(proj, nb, t_len, _split_shift(rw_shift0), pk["rw"], seg,
                     _blockdiag_in(rw_s0, RW_HEADS // 2), t_valid)
    gla_s = jnp.swapaxes(_blockdiag_out(gla_s, GLA_DV, GLA_DK), 2, 3)
    rw_s = _blockdiag_out(rw_s, RW_N, RW_N)
    return oa, oc, gla_s, rw_s


def kernel(x_prompt, x_sample, mem_prompt, cache_sb_k, cache_sb_v, page_table, state_gla, state_rwkv,
           state_rwkv_shift, cache_mem_k, cache_mem_v, ln_g, ln_b, ffn_w1, ffn_w3, ffn_w2, w_in, b_gate,
           gla_wg2, gla_bg, gla_norm_g, sb_bias, rw_mu, rw_w0, rw_w2, rw_a0, rw_a2, rw_g2, rw_kk, rw_ka,
           rw_rk, rw_lnx_g, rw_lnx_b, w_br, w_o, mem_wq, mem_wk, mem_wv, mem_wo):
    bp, tp, d = x_prompt.shape
    db, ts, _ = x_sample.shape
    mlen = mem_prompt.shape[1]
    n_pool = cache_sb_k.shape[1]
    assert tp % CHUNK == 0 and ts <= CHUNK

    lane = jnp.arange(2 * LANES)
    seg = (lane[:, None] // RW_N == lane[None, :] // RW_N).astype(BF16)
    tq_sb = min(SB_TQ, tp)
    sidx = jnp.arange(min(SB_SUB, tq_sb))
    ms = -(sidx[:, None] >= sidx[None, :]).astype(BF16)
    kidx = jnp.arange(PAGE_SIZE)
    ms_s = -(kidx[:, None] > kidx[None, :]).astype(BF16)
    head_lane = (jnp.arange(SB_W)[None, :] // SB_DH == jnp.arange(SB_HEADS)[:, None]).astype(F32)
    cache_kt = cache_sb_k.transpose(0, 1, 3, 4, 2)
    cache_vt = cache_sb_v.transpose(0, 1, 3, 4, 2)
    tpad = 1 << (ts - 1).bit_length()

    yp = x_prompt.reshape(bp * tp, d)
    ys = x_sample.reshape(db * ts, d)
    memp = mem_prompt.reshape(bp * mlen, d)
    outs = {k: [] for k in ("kp", "vp", "gp", "rp", "shp", "mk", "mv", "ks", "vs", "gs", "rs", "shs")}
    for l in range(DEPTH):
        pk = _pack_layer(l, ln_g, ln_b, ffn_w1, ffn_w3, ffn_w2, w_in, b_gate, gla_wg2, gla_bg, gla_norm_g, sb_bias,
                         rw_mu, rw_w0, rw_w2, rw_a0, rw_a2, rw_g2, rw_kk, rw_ka, rw_rk, rw_lnx_g, rw_lnx_b, w_br,
                         w_o, mem_wq, mem_wk, mem_wv, mem_wo)
        mkv = _matmul(memp, pk["mem_wkv"])
        mk, mv = mkv[:, :d].reshape(bp, mlen, d), mkv[:, d:].reshape(bp, mlen, d)
        x1 = _ffn_ln(yp, *pk["ffn"][0], pk["ln_g"][0], pk["ln_b"][0])
        proj = _matmul(x1, pk["w_in"])
        oa, oc, gla_s, rw_s = _mixers_recurrent(
            proj, bp, tp, pk, seg, jnp.zeros((bp, GLA_HEADS, GLA_DK, GLA_DV), F32),
            jnp.zeros((bp, RW_HEADS, RW_N, RW_N), F32), jnp.zeros((bp, RW_COLS), F32), None)
        bias_rows = jnp.broadcast_to(pk["sb_bias"].reshape(SB_HEADS // 2, 2, 1), (SB_HEADS // 2, 2, tq_sb))
        ob = _sb_prompt(proj, bp, tp, bias_rows, ms)
        x2 = _merge_ln(x1, oa, ob, oc, proj, pk["b_gate"], pk["w_br"], pk["w_o"], pk["ln_g"][1], pk["ln_b"][1])
        x3 = _memattn_ln(x2, pk["mem_wq"], pk["mem_wo"], mk, mv, pk["ln_g"][2], pk["ln_b"][2], tp)
        yp = _ffn_ln(x3, *pk["ffn"][1], pk["ln_g"][3], pk["ln_b"][3])
        proj3 = proj.reshape(bp, tp, P_COLS)
        outs["kp"].append(proj[:, COL_SB_K:COL_SB_K + SB_W])
        outs["vp"].append(proj[:, COL_SB_V:COL_SB_V + SB_W])
        outs["gp"].append(gla_s)
        outs["rp"].append(rw_s)
        outs["shp"].append(proj3[:, -1, COL_RW:COL_RW + RW_COLS])
        outs["mk"].append(mk.reshape(bp, mlen, MEM_HEADS, d // MEM_HEADS))
        outs["mv"].append(mv.reshape(bp, mlen, MEM_HEADS, d // MEM_HEADS))
        x1 = _ffn_ln(ys, *pk["ffn"][0], pk["ln_g"][0], pk["ln_b"][0])
        proj = _matmul(x1, pk["w_in"])
        proj3 = proj.reshape(db, ts, P_COLS)
        proj_pad = jnp.pad(proj3, ((0, 0), (0, CHUNK - ts), (0, 0))).reshape(db * CHUNK, P_COLS)
        oa, oc, gla_s, rw_s = _mixers_recurrent(proj_pad, db, CHUNK, pk, seg, state_gla[l], state_rwkv[l],
                                                state_rwkv_shift[l], ts)
        oa = oa.reshape(db, CHUNK, GLA_V)[:, :ts].reshape(db * ts, GLA_V)
        oc = oc.reshape(db, CHUNK, RW_W)[:, :ts].reshape(db * ts, RW_W)
        qs = proj3[:, :, COL_SB_Q:COL_SB_Q + SB_W]
        ks_new = proj3[:, :, COL_SB_K:COL_SB_K + SB_W]
        vs_new = proj3[:, :, COL_SB_V:COL_SB_V + SB_W]
        qpad = jnp.pad(qs, ((0, 0), (0, tpad - ts), (0, 0)))
        qbd = (qpad[:, None, :, :] * head_lane[None, :, None, :]).reshape(db, SB_HEADS * tpad, SB_W)
        pad_page = ((0, 0), (0, 0), (0, PAGE_SIZE - ts))
        bias_col = jnp.repeat(pk["sb_bias"], tpad).reshape(SB_HEADS * tpad, 1)
        ob = _sb_sample(qbd, bias_col, jnp.pad(jnp.swapaxes(ks_new, 1, 2), pad_page),
                        jnp.pad(jnp.swapaxes(vs_new, 1, 2), pad_page), cache_kt, cache_vt, page_table, l, ms_s)
        ob = jnp.einsum("bhtgd,hg->btgd", ob.reshape(db, SB_HEADS, tpad, SB_HEADS, SB_DH)[:, :, :ts],
                        jnp.eye(SB_HEADS, dtype=F32)).reshape(db * ts, SB_W)
        x2 = _merge_ln(x1, oa, ob, oc, proj, pk["b_gate"], pk["w_br"], pk["w_o"], pk["ln_g"][1], pk["ln_b"][1])
        x3 = _memattn_ln(x2, pk["mem_wq"], pk["mem_wo"], cache_mem_k[l].reshape(db, mlen, d),
                         cache_mem_v[l].reshape(db, mlen, d), pk["ln_g"][2], pk["ln_b"][2], ts)
        ys = _ffn_ln(x3, *pk["ffn"][1], pk["ln_g"][3], pk["ln_b"][3])
        outs["ks"].append(ks_new.reshape(db, ts, SB_HEADS, SB_DH))
        outs["vs"].append(vs_new.reshape(db, ts, SB_HEADS, SB_DH))
        outs["gs"].append(gla_s)
        outs["rs"].append(rw_s)
        outs["shs"].append(proj3[:, -1, COL_RW:COL_RW + RW_COLS])
    st = lambda k: jnp.stack(outs[k])
    heads_p = lambda k: st(k).reshape(DEPTH, bp, tp, SB_HEADS, SB_DH)
    return (yp.reshape(bp, tp, d), ys.reshape(db, ts, d), heads_p("kp"), heads_p("vp"), st("gp"), st("rp"), st("shp"),
            st("mk"), st("mv"), st("ks"), st("vs"), st("gs"), st("rs"), st("shs"))
```

```python
import functools
import math

import jax
import jax.numpy as jnp
from jax import lax
from jax.experimental import pallas as pl
from jax.experimental.pallas import tpu as pltpu

F32 = jnp.float32
BF16 = jnp.bfloat16

DEPTH = 4
DN_ALPHA = (2 * DEPTH) ** 0.25
LN_EPS = 1e-5
LOG2E = 1.4426950408889634
FFN_RES = 0.5
GLA_HEADS, GLA_DK, GLA_DV, GLA_LR = 4, 64, 128, 16
GLA_GATE_NORM = 16.0
SB_HEADS, SB_DH = 8, 64
RW_HEADS, RW_N = 8, 64
RW_LR_W, RW_LR_A, RW_LR_G = 64, 64, 128
RW_LN_EPS = 64e-5
N_BRANCH = 3
MEM_HEADS = 4
PAGE_SIZE = 128

GLA_QK = GLA_HEADS * GLA_DK
GLA_V = GLA_HEADS * GLA_DV
SB_W = SB_HEADS * SB_DH
RW_W = RW_HEADS * RW_N
RW_LR = RW_LR_W + RW_LR_A + RW_LR_G
RW_COLS = 3 * RW_W + RW_LR

COL_GLA_Q, COL_GLA_K, COL_GLA_V, COL_GLA_G = 0, 256, 512, 1024
COL_SB_Q, COL_SB_K, COL_SB_V = 1536, 2048, 2560
COL_GATE = 3072
COL_RW = 6144
COL_RW_LR = COL_RW + 3 * RW_W
COL_GDA = COL_RW + RW_COLS
P_COLS = 8192

LANES = 128
CHUNK = 64
VMEM_LIMIT = 56 * 1024 * 1024


def _cparams(sem):
    return pltpu.CompilerParams(dimension_semantics=sem, vmem_limit_bytes=VMEM_LIMIT)


def _iota(shape, dim):
    return lax.broadcasted_iota(jnp.int32, shape, dim)


_DN = {"nn": (((1,), (0,)), ((), ())), "nt": (((1,), (1,)), ((), ())), "tn": (((0,), (0,)), ((), ()))}


def _pdot(a, b, kind="nn"):
    return lax.dot_general(a.astype(BF16), b.astype(BF16), _DN[kind], preferred_element_type=F32)


def _split3(x):
    h = x.astype(BF16)
    r = x - h.astype(F32)
    m = r.astype(BF16)
    lo = (r - m.astype(F32)).astype(BF16)
    return h, m, lo


def _exact_left(mat_bf16, x):
    h, m, lo = _split3(x)
    d = lambda y: jnp.dot(mat_bf16, y, preferred_element_type=F32)
    return d(h) + (d(m) + d(lo))


def _exact_right(x, mat_bf16, terms=3):
    h, m, lo = _split3(x)
    d = lambda y: jnp.dot(y, mat_bf16, preferred_element_type=F32)
    if terms == 2:
        return d(h) + d(m)
    return d(h) + (d(m) + d(lo))


def _seg_sum(x, seg2_bf16):
    w = seg2_bf16.shape[1]
    parts = []
    for j in range(x.shape[1] // w):
        xj = x[:, j * w:(j + 1) * w]
        h = xj.astype(BF16)
        m = (xj - h.astype(F32)).astype(BF16)
        parts.append(jnp.dot(jnp.concatenate([h, m], axis=1), seg2_bf16, preferred_element_type=F32))
    return jnp.concatenate(parts, axis=1)


def _layer_norm(y, g, b):
    mu = jnp.mean(y, axis=-1, keepdims=True)
    d = y - mu
    var = jnp.mean(d * d, axis=-1, keepdims=True)
    return d * lax.rsqrt(var + LN_EPS) * g + b


def _log_sigmoid(x):
    return jnp.minimum(x, 0.0) - jnp.log1p(jnp.exp(-jnp.abs(x)))


def _ffn_ln_body(x_ref, w1_ref, w3_ref, w2_ref, g_ref, b_ref, o_ref, xb_ref, acc_ref):
    j = pl.program_id(1)

    @pl.when(j == 0)
    def _():
        xb_ref[...] = x_ref[...].astype(BF16)
        acc_ref[...] = jnp.zeros_like(acc_ref)

    xb = xb_ref[...]
    h1 = jnp.dot(xb, w1_ref[...], preferred_element_type=F32)
    h3 = jnp.dot(xb, w3_ref[...], preferred_element_type=F32)
    h = (h1 * jax.nn.sigmoid(h1)) * h3
    acc_ref[...] += jnp.dot(h.astype(BF16), w2_ref[...], preferred_element_type=F32)

    @pl.when(j == pl.num_programs(1) - 1)
    def _():
        y = DN_ALPHA * x_ref[...] + FFN_RES * acc_ref[...]
        o_ref[...] = _layer_norm(y, g_ref[...], b_ref[...])


def _ffn_ln(x, w1, w3, w2, g, b):
    n, d = x.shape
    dff = w1.shape[1]
    tm = min(n, 512)
    tf = dff // 2 if (dff // 2) % LANES == 0 else 256
    assert n % tm == 0 and dff % tf == 0
    return pl.pallas_call(
        _ffn_ln_body,
        grid=(n // tm, dff // tf),
        in_specs=[
            pl.BlockSpec((tm, d), lambda i, j: (i, 0)),
            pl.BlockSpec((d, tf), lambda i, j: (0, j)),
            pl.BlockSpec((d, tf), lambda i, j: (0, j)),
            pl.BlockSpec((tf, d), lambda i, j: (j, 0)),
            pl.BlockSpec((1, d), lambda i, j: (0, 0)),
            pl.BlockSpec((1, d), lambda i, j: (0, 0)),
        ],
        out_specs=pl.BlockSpec((tm, d), lambda i, j: (i, 0)),
        out_shape=jax.ShapeDtypeStruct((n, d), F32),
        scratch_shapes=[pltpu.VMEM((tm, d), BF16), pltpu.VMEM((tm, d), F32)],
        compiler_params=_cparams(("parallel", "arbitrary")),
        name="ffn_ln",
    )(x, w1, w3, w2, g, b)


def _mm_body(x_ref, w_ref, o_ref, *rest, tap_block):
    xb_ref = rest[-1]
    j = pl.program_id(1)

    @pl.when(j == 0)
    def _():
        xb_ref[...] = x_ref[...].astype(BF16)

    r = jnp.dot(xb_ref[...], w_ref[...], preferred_element_type=F32)
    o_ref[...] = r
    if tap_block is not None:
        @pl.when(j == tap_block)
        def _():
            rest[0][...] = r


def _matmul(x, w, tap_block=None):
    n, k = x.shape
    m = w.shape[1]
    tm = min(n, 1024)
    tn = min(m, 1024)
    assert n % tm == 0 and m % tn == 0
    out_specs = [pl.BlockSpec((tm, tn), lambda i, j: (i, j))]
    out_shape = [jax.ShapeDtypeStruct((n, m), F32)]
    if tap_block is not None:
        out_specs.append(pl.BlockSpec((tm, tn), lambda i, j: (i, 0)))
        out_shape.append(jax.ShapeDtypeStruct((n, tn), F32))
    res = pl.pallas_call(
        functools.partial(_mm_body, tap_block=tap_block),
        grid=(n // tm, m // tn),
        in_specs=[
            pl.BlockSpec((tm, k), lambda i, j: (i, 0)),
            pl.BlockSpec((k, tn), lambda i, j: (0, j)),
        ],
        out_specs=out_specs,
        out_shape=out_shape,
        scratch_shapes=[pltpu.VMEM((tm, k), BF16)],
        compiler_params=_cparams(("parallel", "arbitrary")),
        name="proj_matmul",
    )(x, w)
    return res if tap_block is not None else res[0]


def _merge_body(x_ref, oa_ref, ob_ref, oc_ref, pg0_ref, pg1_ref, pg2_ref, bg_ref, wbr_ref, wo_ref,
                g_ref, b_ref, o_ref):
    s = None
    for n, (o_r, pg_r) in enumerate(((oa_ref, pg0_ref), (ob_ref, pg1_ref), (oc_ref, pg2_ref))):
        br = jnp.dot(o_r[...].astype(BF16), wbr_ref[n], preferred_element_type=F32)
        term = jax.nn.sigmoid(pg_r[...] + bg_ref[n]) * br
        s = term if s is None else s + term
    mix = jnp.dot(s.astype(BF16), wo_ref[...], preferred_element_type=F32)
    o_ref[...] = _layer_norm(DN_ALPHA * x_ref[...] + mix, g_ref[...], b_ref[...])


def _merge_ln(x, oa, ob, oc, proj, b_gate, w_br, w_o, g, b):
    n, d = x.shape
    bw = oa.shape[1]
    tm = min(n, 512)
    gate_blk = COL_GATE // d
    row = lambda i: (i, 0)
    return pl.pallas_call(
        _merge_body,
        grid=(n // tm,),
        in_specs=[
            pl.BlockSpec((tm, d), row),
            pl.BlockSpec((tm, bw), row),
            pl.BlockSpec((tm, bw), row),
            pl.BlockSpec((tm, bw), row),
            pl.BlockSpec((tm, d), lambda i: (i, gate_blk)),
            pl.BlockSpec((tm, d), lambda i: (i, gate_blk + 1)),
            pl.BlockSpec((tm, d), lambda i: (i, gate_blk + 2)),
            pl.BlockSpec((N_BRANCH, 1, d), lambda i: (0, 0, 0)),
            pl.BlockSpec((N_BRANCH, bw, d), lambda i: (0, 0, 0)),
            pl.BlockSpec((d, d), lambda i: (0, 0)),
            pl.BlockSpec((1, d), lambda i: (0, 0)),
            pl.BlockSpec((1, d), lambda i: (0, 0)),
        ],
        out_specs=pl.BlockSpec((tm, d), row),
        out_shape=jax.ShapeDtypeStruct((n, d), F32),
        compiler_params=_cparams(("parallel",)),
        name="merge_ln",
    )(x, oa, ob, oc, proj, proj, proj, b_gate, w_br, w_o, g, b)


def _memattn_body(x_ref, wq_ref, wo_ref, mk_ref, mv_ref, g_ref, b_ref, o_ref, q_ref, acc_ref, *,
                  t_seq, tiles_per_batch, masked):
    i = pl.program_id(0)
    j = pl.program_id(1)
    tm, d = x_ref.shape
    dh = d // MEM_HEADS

    @pl.when(j == 0)
    def _():
        q_ref[...] = jnp.dot(x_ref[...].astype(BF16), wq_ref[...], preferred_element_type=F32).astype(BF16)
        acc_ref[...] = jnp.zeros_like(acc_ref)

    if masked:
        row = i * tm + _iota((tm, 1), 0)
        mem_b = i // tiles_per_batch + j
        keep = (row >= mem_b * t_seq) & (row < (mem_b + 1) * t_seq)
    sls = [slice(h * dh, (h + 1) * dh) for h in range(MEM_HEADS)]
    ss = [lax.dot_general(q_ref[:, sl], mk_ref[0, :, sl].astype(BF16), _DN["nt"], preferred_element_type=F32)
          * (dh ** -0.5) for sl in sls]
    es = [jnp.exp(s - jnp.max(s, axis=-1, keepdims=True)) for s in ss]
    atts = [e / jnp.sum(e, axis=-1, keepdims=True) for e in es]
    ohs = [jnp.dot(att.astype(BF16), mv_ref[0, :, sl].astype(BF16), preferred_element_type=F32)
           for att, sl in zip(atts, sls)]
    for oh, sl in zip(ohs, sls):
        if masked:
            oh = jnp.where(keep, oh, 0.0)
        acc_ref[:, sl] += oh

    @pl.when(j == pl.num_programs(1) - 1)
    def _():
        xm = jnp.dot(acc_ref[...].astype(BF16), wo_ref[...], preferred_element_type=F32)
        o_ref[...] = _layer_norm(DN_ALPHA * x_ref[...] + xm, g_ref[...], b_ref[...])


def _memattn_ln(x, wq, wo, mem_k, mem_v, g, b, t_seq):
    n, d = x.shape
    nb, m, _ = mem_k.shape
    tm = min(n, 512)
    if t_seq >= tm:
        assert t_seq % tm == 0
        tiles_per_batch, nj, masked = t_seq // tm, 1, False
    else:
        assert tm % t_seq == 0 and n == tm
        tiles_per_batch, nj, masked = 1, nb, True
    body = functools.partial(_memattn_body, t_seq=t_seq, tiles_per_batch=tiles_per_batch, masked=masked)
    mem_map = lambda i, j: (i // tiles_per_batch + j, 0, 0)
    return pl.pallas_call(
        body,
        grid=(n // tm, nj),
        in_specs=[
            pl.BlockSpec((tm, d), lambda i, j: (i, 0)),
            pl.BlockSpec((d, d), lambda i, j: (0, 0)),
            pl.BlockSpec((d, d), lambda i, j: (0, 0)),
            pl.BlockSpec((1, m, d), mem_map),
            pl.BlockSpec((1, m, d), mem_map),
            pl.BlockSpec((1, d), lambda i, j: (0, 0)),
            pl.BlockSpec((1, d), lambda i, j: (0, 0)),
        ],
        out_specs=pl.BlockSpec((tm, d), lambda i, j: (i, 0)),
        out_shape=jax.ShapeDtypeStruct((n, d), F32),
        scratch_shapes=[pltpu.VMEM((tm, d), BF16), pltpu.VMEM((tm, d), F32)],
        compiler_params=_cparams(("parallel", "arbitrary")),
        name="memattn_ln",
    )(x, wq, wo, mem_k, mem_v, g, b)


def _stack2(x):
    return jnp.concatenate([x, x], axis=0)


def _head_rows_mask(c, width, seg):
    r = _iota((2 * c, width), 0)
    l = _iota((2 * c, width), 1)
    return ((r < c) & (l < seg)) | ((r >= c) & (l >= seg))


GLA_CHUNKS_PER_ITER = 4


def _gla_body(q_ref, k_ref, v_ref, g_ref, gda_ref, wg2_ref, bg_ref, ng_ref, s0_ref, o_ref, sout_ref, s_ref, *,
              t_valid):
    t = pl.program_id(1)
    tt = q_ref.shape[0]
    c = CHUNK
    npair = GLA_HEADS // 2
    kw, vw = 2 * GLA_DK, 2 * GLA_DV

    @pl.when(t == 0)
    def _():
        s_ref[...] = s0_ref[0]

    ri = _iota((c, c), 0)
    ci = _iota((c, c), 1)
    tri_incl = (ci <= ri).astype(F32).astype(BF16)
    r2 = _iota((2 * c, 2 * c), 0)
    c2 = _iota((2 * c, 2 * c), 1)
    same_head = ((r2 < c) & (c2 < c)) | ((r2 >= c) & (c2 >= c))
    att_mask = same_head & ((c2 & (c - 1)) <= (r2 & (c - 1)))
    qmask = _head_rows_mask(c, kw, GLA_DK)
    omask = _head_rows_mask(c, vw, GLA_DV)
    sr = _iota((vw, kw), 0)
    sc = _iota((vw, kw), 1)
    bd_mask = ((sr < GLA_DV) & (sc < GLA_DK)) | ((sr >= GLA_DV) & (sc >= GLA_DK))

    cpi = min(GLA_CHUNKS_PER_ITER, tt // c)
    pairs = range(npair)
    kls = [slice(p * kw, (p + 1) * kw) for p in pairs]
    vls = [slice(p * vw, (p + 1) * vw) for p in pairs]

    def chunk_group(ig, carry):
        units = [(j, p) for j in range(cpi) for p in pairs]
        grows = pl.ds(pl.multiple_of(ig * (cpi * c), cpi * c), cpi * c)
        rows = [pl.ds(pl.multiple_of((ig * cpi + j) * c, c), c) for j in range(cpi)]
        la = _log_sigmoid(_pdot(gda_ref[grows, :], wg2_ref[...]) + bg_ref[...]) / GLA_GATE_NORM
        kk = k_ref[grows, :]
        if t_valid is not None:
            valid = (t * tt + ig * (cpi * c) + _iota((cpi * c, 1), 0)) < t_valid
            la = jnp.where(valid, la, 0.0)
            kk = jnp.where(valid, kk, 0.0)
        la3 = _split3(la)
        bcs = [sum(jnp.dot(tri_incl, part[j * c:(j + 1) * c], preferred_element_type=F32) for part in la3)
               for j in range(cpi)]
        b_last = [bc[c - 1:c, :] for bc in bcs]
        e_last = [jnp.exp(b) for b in b_last]
        qs = q_ref[grows, :] * (GLA_DK ** -0.5)
        q_dec = [qs[j * c:(j + 1) * c] * jnp.exp(bcs[j]) for j in range(cpi)]
        k_inv = [kk[j * c:(j + 1) * c] * jnp.exp(-bcs[j]) for j in range(cpi)]
        k_end = [kk[j * c:(j + 1) * c] * jnp.exp(b_last[j] - bcs[j]) for j in range(cpi)]
        vv = [v_ref[rows[j], :] for j in range(cpi)]
        qd2 = {(j, p): jnp.where(qmask, _stack2(q_dec[j][:, kls[p]]), 0.0).astype(BF16) for j, p in units}
        att = {(j, p): jnp.where(att_mask, _pdot(qd2[j, p], _stack2(k_inv[j][:, kls[p]]), "nt"), 0.0)
               for j, p in units}
        intra = {(j, p): jnp.where(omask, _pdot(att[j, p], _stack2(vv[j][:, vls[p]])), 0.0) for j, p in units}
        s_cur = [s_ref[p] for p in pairs]
        for j in range(cpi):
            o2 = [_pdot(qd2[j, p], s_cur[p], "nt") + intra[j, p] for p in pairs]
            upd = [_pdot(vv[j][:, vls[p]], k_end[j][:, kls[p]], "tn") for p in pairs]
            s_cur = [s_cur[p] * e_last[j][:, kls[p]] + jnp.where(bd_mask, upd[p], 0.0) for p in pairs]
            normed = []
            for p in pairs:
                op = o2[p][:c] + o2[p][c:]
                for h in range(2):
                    oh = op[:, h * GLA_DV:(h + 1) * GLA_DV]
                    normed.append(oh * lax.rsqrt(jnp.mean(oh * oh, axis=-1, keepdims=True) + LN_EPS))
            o = jnp.concatenate(normed, axis=1) * ng_ref[...]
            gg = g_ref[rows[j], :]
            o_ref[rows[j], :] = o * (gg * jax.nn.sigmoid(gg))
        for p in pairs:
            s_ref[p] = s_cur[p]
        return carry

    lax.fori_loop(0, tt // (c * cpi), chunk_group, 0)

    @pl.when(t == pl.num_programs(1) - 1)
    def _():
        sout_ref[0] = s_ref[...]


def _gla(proj, nb, t_len, wg2p, bg, ng, s0bd, t_valid):
    tt = min(t_len, 512)
    nt = t_len // tt
    npair = GLA_HEADS // 2
    kw, vw = 2 * GLA_DK, 2 * GLA_DV
    body = functools.partial(_gla_body, t_valid=t_valid)

    def col(width, off):
        blk = off // width
        return pl.BlockSpec((tt, width), lambda b, t: (b * nt + t, blk))

    const2 = lambda b, t: (0, 0)
    return pl.pallas_call(
        body,
        grid=(nb, nt),
        in_specs=[
            col(GLA_QK, COL_GLA_Q), col(GLA_QK, COL_GLA_K), col(GLA_V, COL_GLA_V), col(GLA_V, COL_GLA_G),
            col(LANES, COL_GDA),
            pl.BlockSpec((LANES, GLA_QK), const2),
            pl.BlockSpec((1, GLA_QK), const2),
            pl.BlockSpec((1, GLA_V), const2),
            pl.BlockSpec((1, npair, vw, kw), lambda b, t: (b, 0, 0, 0)),
        ],
        out_specs=[
            pl.BlockSpec((tt, GLA_V), lambda b, t: (b * nt + t, 0)),
            pl.BlockSpec((1, npair, vw, kw), lambda b, t: (b, 0, 0, 0)),
        ],
        out_shape=[
            jax.ShapeDtypeStruct((nb * t_len, GLA_V), F32),
            jax.ShapeDtypeStruct((nb, npair, vw, kw), F32),
        ],
        scratch_shapes=[pltpu.VMEM((npair, vw, kw), F32)],
        compiler_params=_cparams(("parallel", "arbitrary")),
        name="gla",
    )(proj, proj, proj, proj, proj, wg2p, bg, ng, s0bd)


RW_CHUNKS_PER_ITER = 4


def _rwkv_body(r_ref, k_ref, v_ref, lr_ref, shr_ref, shk_ref, shv_ref, shlr_ref, mur_ref, muk_ref, muv_ref,
               mulr_ref, w0_ref, w2_ref, a0_ref, a2_ref, g2_ref, kkp_ref, ka_ref, rk_ref, lng_ref, lnb_ref,
               seg_ref, s0_ref, o_ref, sout_ref,
               s_ref, cr_ref, ck_ref, cv_ref, clr_ref, rs_ref, ws_ref, ks_ref, vs_ref, as_ref, bs_ref, ys_ref, *,
               t_valid):
    t = pl.program_id(1)
    tt = r_ref.shape[0]
    c = CHUNK
    npair = RW_HEADS // 2
    pw = 2 * RW_N

    @pl.when(t == 0)
    def _():
        s_ref[...] = s0_ref[0]
        cr_ref[...] = shr_ref[0]
        ck_ref[...] = shk_ref[0]
        cv_ref[...] = shv_ref[0]
        clr_ref[...] = shlr_ref[0]

    def lerp(x_ref, carry_ref, mu_ref):
        x = x_ref[...]
        prev = pltpu.roll(x, 1, 0)
        prev = jnp.where(_iota(x.shape, 0) == 0, carry_ref[...], prev)
        carry_ref[...] = x_ref[pl.ds(tt - 1, 1), :]
        return x + (prev - x) * mu_ref[...]

    xr = lerp(r_ref, cr_ref, mur_ref)
    xk = lerp(k_ref, ck_ref, muk_ref)
    xv = lerp(v_ref, cv_ref, muv_ref)
    xlr = lerp(lr_ref, clr_ref, mulr_ref)
    w_log = _log_sigmoid(w0_ref[...] + _pdot(jnp.tanh(xlr), w2_ref[...])) - 0.5
    wdec = -jnp.exp(w_log)
    a = jax.nn.sigmoid(a0_ref[...] + _pdot(xlr, a2_ref[...]))
    g = _pdot(jax.nn.sigmoid(xlr), g2_ref[...])
    kk = xk * kkp_ref[...]
    kkn = kk / jnp.maximum(jnp.sqrt(_seg_sum(kk * kk, seg_ref[...])), 1e-12)
    kmod = xk * (1.0 + (a - 1.0) * ka_ref[...])
    av = -kkn
    bv = kkn * a
    if t_valid is not None:
        valid = (t * tt + _iota((tt, 1), 0)) < t_valid
        wdec = jnp.where(valid, wdec, 0.0)
        av = jnp.where(valid, av, 0.0)
        bv = jnp.where(valid, bv, 0.0)
        kmod = jnp.where(valid, kmod, 0.0)
        xv = jnp.where(valid, xv, 0.0)
    rs_ref[...] = xr
    ws_ref[...] = wdec
    ks_ref[...] = kmod
    vs_ref[...] = xv
    as_ref[...] = av
    bs_ref[...] = bv

    ri = _iota((c, c), 0)
    ci = _iota((c, c), 1)
    tri_incl = (ci <= ri).astype(F32).astype(BF16)
    r2 = _iota((2 * c, 2 * c), 0)
    c2 = _iota((2 * c, 2 * c), 1)
    same_head = ((r2 < c) & (c2 < c)) | ((r2 >= c) & (c2 >= c))
    strict = same_head & ((c2 & (c - 1)) < (r2 & (c - 1)))
    incl = same_head & ((c2 & (c - 1)) <= (r2 & (c - 1)))
    eye = (r2 == c2).astype(F32)
    hmask = _head_rows_mask(c, pw, RW_N)
    bd_mask = _head_rows_mask(RW_N, pw, RW_N)

    cpi = min(RW_CHUNKS_PER_ITER, tt // c)
    pairs = range(npair)
    lanes = [slice(p * pw, (p + 1) * pw) for p in pairs]
    units = [(j, p) for j in range(cpi) for p in pairs]

    def chunk_group(ig, carry):
        rows, b_e, k_e, vv, e_last = [], [], [], [], []
        ar4, bk4, v2 = {}, {}, {}
        for j in range(cpi):
            rows.append(pl.ds(pl.multiple_of((ig * cpi + j) * c, c), c))
            w = ws_ref[rows[j], :]
            cs = _exact_left(tri_incl, w)
            c_last = cs[c - 1:c, :]
            e_neg = jnp.exp(-cs)
            e_end = jnp.exp(c_last - cs)
            e_last.append(jnp.exp(c_last))
            a_t = as_ref[rows[j], :] * jnp.exp(cs - w)
            r_t = rs_ref[rows[j], :] * jnp.exp(cs)
            bb = bs_ref[rows[j], :]
            kc = ks_ref[rows[j], :]
            b_t = bb * e_neg
            k_t = kc * e_neg
            b_e.append(bb * e_end)
            k_e.append(kc * e_end)
            vv.append(vs_ref[rows[j], :])
            for p in pairs:
                ln = lanes[p]
                ar4[j, p] = jnp.concatenate([jnp.where(hmask, _stack2(a_t[:, ln]), 0.0),
                                             jnp.where(hmask, _stack2(r_t[:, ln]), 0.0)], axis=0).astype(BF16)
                bk4[j, p] = jnp.concatenate([_stack2(b_t[:, ln]), _stack2(k_t[:, ln])], axis=0).astype(BF16)
                v2[j, p] = _stack2(vv[j][:, ln]).astype(BF16)
        g4 = {u: _pdot(ar4[u], bk4[u], "nt") for u in units}
        n_ab = {u: jnp.where(strict, g4[u][:2 * c, :2 * c], 0.0) for u in units}
        n_ak = {u: jnp.where(strict, g4[u][:2 * c, 2 * c:], 0.0) for u in units}
        n_r = {u: jnp.concatenate([jnp.where(incl, g4[u][2 * c:, :2 * c], 0.0),
                                   jnp.where(incl, g4[u][2 * c:, 2 * c:], 0.0)], axis=1).astype(BF16)
               for u in units}
        nakv = {u: jnp.where(hmask, _pdot(n_ak[u], v2[u]), 0.0) for u in units}
        tinv = {u: eye + n_ab[u] for u in units}
        x = n_ab
        for _ in range(int(math.log2(c)) - 1):
            x = {u: _pdot(x[u], x[u]) for u in units}
            tinv = {u: tinv[u] + _pdot(tinv[u], x[u]) for u in units}
        s_cur = [s_ref[p] for p in pairs]
        for j in range(cpi):
            as4 = [_pdot(ar4[j, p], s_cur[p], "nt") for p in pairs]
            u2 = [_pdot(tinv[j, p], as4[p][:2 * c] + nakv[j, p]) for p in pairs]
            y2 = [as4[p][2 * c:] + jnp.where(
                hmask, _pdot(n_r[j, p], jnp.concatenate([u2[p].astype(BF16), v2[j, p]], axis=0)), 0.0)
                  for p in pairs]
            upd = [_pdot(jnp.concatenate([u2[p][:c] + u2[p][c:], vv[j][:, lanes[p]]], axis=0),
                         jnp.concatenate([b_e[j][:, lanes[p]], k_e[j][:, lanes[p]]], axis=0), "tn") for p in pairs]
            s_cur = [s_cur[p] * e_last[j][:, lanes[p]] + jnp.where(bd_mask, upd[p], 0.0) for p in pairs]
            ys_ref[rows[j], :] = jnp.concatenate([y2[p][:c] + y2[p][c:] for p in pairs], axis=1)
        for p in pairs:
            s_ref[p] = s_cur[p]
        return carry

    lax.fori_loop(0, tt // (c * cpi), chunk_group, 0)

    y = ys_ref[...]
    seg = seg_ref[...]
    inv_n = 1.0 / RW_N
    mu = _seg_sum(y, seg) * inv_n
    d = y - mu
    var = _seg_sum(d * d, seg) * inv_n
    yn = d * lax.rsqrt(var + RW_LN_EPS) * lng_ref[...] + lnb_ref[...]
    bonus = _seg_sum(xr * kmod * rk_ref[...], seg)
    o_ref[...] = (yn + bonus * xv) * g

    @pl.when(t == pl.num_programs(1) - 1)
    def _():
        sout_ref[0] = s_ref[...]


def _rwkv(proj, nb, t_len, shifts, prm, seg, s0bd, t_valid):
    tt = min(t_len, 256)
    nt = t_len // tt
    npair = RW_HEADS // 2
    pw = 2 * RW_N
    body = functools.partial(_rwkv_body, t_valid=t_valid)

    def col(width, off):
        blk = off // width
        return pl.BlockSpec((tt, width), lambda b, t: (b * nt + t, blk))

    def per_batch(width):
        return pl.BlockSpec((1, 1, width), lambda b, t: (b, 0, 0))

    def const(shape):
        return pl.BlockSpec(shape, lambda b, t: (0,) * len(shape))

    vec = const((1, RW_W))
    lrm = const((RW_LR, RW_W))
    state = pl.BlockSpec((1, npair, pw, pw), lambda b, t: (b, 0, 0, 0))
    tile = lambda: pltpu.VMEM((tt, RW_W), F32)
    return pl.pallas_call(
        body,
        grid=(nb, nt),
        in_specs=[
            col(RW_W, COL_RW), col(RW_W, COL_RW + RW_W), col(RW_W, COL_RW + 2 * RW_W), col(RW_LR, COL_RW_LR),
            per_batch(RW_W), per_batch(RW_W), per_batch(RW_W), per_batch(RW_LR),
            vec, vec, vec, const((1, RW_LR)),
            vec, lrm, vec, lrm, lrm, vec, vec, vec, vec, vec,
            const(seg.shape),
            state,
        ],
        out_specs=[pl.BlockSpec((tt, RW_W), lambda b, t: (b * nt + t, 0)), state],
        out_shape=[
            jax.ShapeDtypeStruct((nb * t_len, RW_W), F32),
            jax.ShapeDtypeStruct((nb, npair, pw, pw), F32),
        ],
        scratch_shapes=[
            pltpu.VMEM((npair, pw, pw), F32),
            pltpu.VMEM((1, RW_W), F32), pltpu.VMEM((1, RW_W), F32), pltpu.VMEM((1, RW_W), F32),
            pltpu.VMEM((1, RW_LR), F32),
            tile(), tile(), tile(), tile(), tile(), tile(), tile(),
        ],
        compiler_params=_cparams(("parallel", "arbitrary")),
        name="rwkv7",
    )(proj, proj, proj, proj, *shifts, *prm, seg, s0bd)


SB_TQ = 512
SB_SUB = 256


def _sbp_body(q_ref, k_ref, v_ref, bias_ref, ms_ref, o_ref):
    i = pl.program_id(2)
    tq = q_ref.shape[0]
    tk = tq
    sub = min(SB_SUB, tk)
    nsub = tk // sub
    ms = ms_ref[...]
    lane = _iota((1, LANES), 1)
    q = q_ref[...] * (SB_DH ** -0.5 * LOG2E)
    causal = _iota((tq, tk), 1) < _iota((tq, tk), 0)
    heads = range(2)
    hms = [(lane >= h * SB_DH) & (lane < (h + 1) * SB_DH) for h in heads]
    qhs = [jnp.where(hms[h], q, 0.0).astype(BF16) for h in heads]
    biases = [bias_ref[0, h:h + 1, :] * LOG2E for h in heads]
    units = [(h, slice(j * sub, (j + 1) * sub)) for h in heads for j in reversed(range(nsub))]

    def block(kb, carry, masked):
        accs, laters = list(carry[0]), list(carry[1])
        rows = pl.ds(pl.multiple_of(kb * tk, tk), tk)
        kblk = k_ref[rows, :].astype(BF16)
        vblk = v_ref[rows, :].astype(BF16)
        z = [lax.dot_general(qhs[h], kblk, _DN["nt"], preferred_element_type=F32) + biases[h] for h in heads]
        sp = [jnp.maximum(zh, 0.0) + jnp.log2(1.0 + jnp.exp2(-jnp.abs(zh))) for zh in z]
        if masked:
            sp = [jnp.where(causal, s, 0.0) for s in sp]
        ws = [jnp.exp2(z[h][:, sl] + jnp.dot(sp[h][:, sl].astype(BF16), ms, preferred_element_type=F32))
              for h, sl in units]
        if masked:
            ws = [jnp.where(causal[:, sl], w, 0.0) for (h, sl), w in zip(units, ws)]
        pvs = [jnp.dot(w.astype(BF16), vblk[sl], preferred_element_type=F32) for (h, sl), w in zip(units, ws)]
        tots = [jnp.sum(sp[h][:, sl], axis=1, keepdims=True) for h, sl in units]
        for (h, sl), pv, tot in zip(units, pvs, tots):
            accs[h] = accs[h] + jnp.exp2(laters[h]) * pv
            laters[h] = laters[h] - tot
        return tuple(accs), tuple(laters)

    zero = (tuple(jnp.zeros((tq, LANES), F32) for _ in heads), tuple(jnp.zeros((tq, 1), F32) for _ in heads))
    accs, _ = lax.fori_loop(0, i, lambda n, cr: block(i - 1 - n, cr, False), block(i, zero, True))
    o_ref[...] = sum(jnp.where(hms[h], accs[h], 0.0) for h in heads)


def _sb_prompt(proj, nb, t_len, bias_rows, ms):
    tq = min(SB_TQ, t_len)
    nq = t_len // tq
    npair = SB_HEADS // 2
    qb, kb, vb = COL_SB_Q // LANES, COL_SB_K // LANES, COL_SB_V // LANES
    return pl.pallas_call(
        _sbp_body,
        grid=(nb, npair, nq),
        in_specs=[
            pl.BlockSpec((tq, LANES), lambda b, p, i: (b * nq + i, qb + p)),
            pl.BlockSpec((t_len, LANES), lambda b, p, i: (b, kb + p)),
            pl.BlockSpec((t_len, LANES), lambda b, p, i: (b, vb + p)),
            pl.BlockSpec((1, 2, tq), lambda b, p, i: (p, 0, 0)),
            pl.BlockSpec(ms.shape, lambda b, p, i: (0, 0)),
        ],
        out_specs=pl.BlockSpec((tq, LANES), lambda b, p, i: (b * nq + i, p)),
        out_shape=jax.ShapeDtypeStruct((nb * t_len, SB_W), F32),
        compiler_params=_cparams(("parallel", "parallel", "arbitrary")),
        name="sb_prompt",
    )(proj, proj, proj, bias_rows, ms)


SB_PAGES_PER_STEP = 16


def _sbs_body(pt_ref, q_ref, bias_ref, knew_ref, vnew_ref, *rest):
    g_pages = SB_PAGES_PER_STEP
    k_refs = rest[:g_pages]
    v_refs = rest[g_pages:2 * g_pages]
    ms_ref, o_ref, acc_ref, later_ref = rest[2 * g_pages:]
    s = pl.program_id(1)
    nrow, width = q_ref.shape[1:]
    nkey = knew_ref.shape[2]
    tpad = nrow // SB_HEADS
    q2 = (q_ref[0] * (SB_DH ** -0.5)).astype(BF16)
    bias = bias_ref[...]
    ms = ms_ref[...]
    fresh_ok = _iota((nrow, nkey), 1) < (_iota((nrow, nkey), 0) & (tpad - 1))

    def page_terms(kt, keep):
        z = jnp.dot(q2, kt.astype(BF16), preferred_element_type=F32) + bias
        sp = jnp.maximum(z, 0.0) + jnp.log1p(jnp.exp(-jnp.abs(z)))
        zs = z - sp
        if keep is not None:
            sp = jnp.where(keep, sp, 0.0)
        return zs, _exact_right(sp, ms, terms=2), jnp.sum(sp, axis=1, keepdims=True)

    def combine(pages, keep, carry):
        acc, later = carry
        for (zs, cum, tot), vt in pages:
            w = jnp.exp(zs + (cum + later))
            later = later - tot
            if keep is not None:
                w = jnp.where(keep, w, 0.0)
            acc = acc + lax.dot_general(w.astype(BF16), vt.astype(BF16), _DN["nt"], preferred_element_type=F32)
        return acc, later

    @pl.when(s == 0)
    def _():
        zero = (jnp.zeros(acc_ref.shape, F32), jnp.zeros(later_ref.shape, F32))
        acc_ref[...], later_ref[...] = combine([(page_terms(knew_ref[0], fresh_ok), vnew_ref[0])], fresh_ok, zero)

    @pl.when(s > 0)
    def _():
        flat = lambda ref: ref[0, 0].reshape(width, nkey)
        pages = [(page_terms(flat(k_refs[g]), None), flat(v_refs[g])) for g in range(g_pages)]
        acc_ref[...], later_ref[...] = combine(pages, None, (acc_ref[...], later_ref[...]))

    @pl.when(s == pl.num_programs(1) - 1)
    def _():
        o_ref[0] = acc_ref[...]


def _sb_sample(qbd, bias_col, knew_t, vnew_t, cache_kt, cache_vt, page_table, layer, ms):
    nb, nrow, width = qbd.shape
    nkey = knew_t.shape[2]
    n_pages = page_table.shape[1]
    g_pages = SB_PAGES_PER_STEP
    assert n_pages % g_pages == 0
    nsteps = n_pages // g_pages

    def page_spec(g):
        def imap(b, s, pt):
            page = n_pages - 1 - (jnp.maximum(s, 1) - 1) * g_pages - g
            return (layer, pt[b, page], 0, 0, 0)
        return pl.BlockSpec((1, 1) + cache_kt.shape[2:], imap)

    per_b = lambda b, s, pt: (b, 0, 0)
    grid_spec = pltpu.PrefetchScalarGridSpec(
        num_scalar_prefetch=1,
        grid=(nb, nsteps + 1),
        in_specs=[
            pl.BlockSpec((1, nrow, width), per_b),
            pl.BlockSpec((nrow, 1), lambda b, s, pt: (0, 0)),
            pl.BlockSpec((1, width, nkey), per_b),
            pl.BlockSpec((1, width, nkey), per_b),
            *[page_spec(g) for g in range(g_pages)],
            *[page_spec(g) for g in range(g_pages)],
            pl.BlockSpec(ms.shape, lambda b, s, pt: (0, 0)),
        ],
        out_specs=pl.BlockSpec((1, nrow, width), per_b),
        scratch_shapes=[pltpu.VMEM((nrow, width), F32), pltpu.VMEM((nrow, 1), F32)],
    )
    return pl.pallas_call(
        _sbs_body,
        grid_spec=grid_spec,
        out_shape=jax.ShapeDtypeStruct((nb, nrow, width), F32),
        compiler_params=_cparams(("parallel", "arbitrary")),
        name="sb_sample",
    )(page_table, qbd, bias_col, knew_t, vnew_t, *([cache_kt] * g_pages), *([cache_vt] * g_pages), ms)


def _blockdiag_in(s, pairs):
    b, h, do, di = s.shape
    s = s.reshape(b, pairs, 2, do, di)
    return jnp.einsum("bphvk,hg->bphvgk", s, jnp.eye(2, dtype=s.dtype)).reshape(b, pairs, 2 * do, 2 * di)


def _blockdiag_out(sbd, do, di):
    b, pairs = sbd.shape[:2]
    x = sbd.reshape(b, pairs, 2, do, 2, di)
    return jnp.einsum("bphvhk->bphvk", x).reshape(b, 2 * pairs, do, di)


def _pack_layer(l, ln_g, ln_b, ffn_w1, ffn_w3, ffn_w2, w_in, b_gate, gla_wg2, gla_bg, gla_norm_g, sb_bias, rw_mu,
                rw_w0, rw_w2, rw_a0, rw_a2, rw_g2, rw_kk, rw_ka, rw_rk, rw_lnx_g, rw_lnx_b, w_br, w_o, mem_wq,
                mem_wk, mem_wv, mem_wo):
    d = w_in.shape[1]
    gla_cols = 2 * GLA_QK + 2 * GLA_V + GLA_LR
    o_sb = gla_cols
    o_rw = o_sb + 3 * SB_W
    o_gate = o_rw + RW_COLS
    w = w_in[l]
    w_re = jnp.concatenate([
        w[:, :gla_cols - GLA_LR], w[:, o_sb:o_rw], w[:, o_gate:], w[:, o_rw:o_gate],
        w[:, gla_cols - GLA_LR:gla_cols], jnp.zeros((d, P_COLS - COL_GDA - GLA_LR), F32)], axis=1)
    assert w_re.shape[1] == P_COLS
    row = lambda v: v.reshape(1, -1)
    mu = rw_mu[l]

    def lr_pad(m, off):
        return jnp.zeros((RW_LR, RW_W), F32).at[off:off + m.shape[0]].set(m).astype(BF16)

    return dict(
        ln_g=[row(ln_g[l, i]) for i in range(4)], ln_b=[row(ln_b[l, i]) for i in range(4)],
        ffn=[(ffn_w1[l, i].astype(BF16), ffn_w3[l, i].astype(BF16), ffn_w2[l, i].astype(BF16)) for i in range(2)],
        w_in=w_re.astype(BF16),
        b_gate=b_gate[l].reshape(N_BRANCH, 1, -1), w_br=w_br[l].astype(BF16), w_o=w_o[l].astype(BF16),
        gla_wg2=jnp.zeros((LANES, GLA_QK), F32).at[:GLA_LR].set(gla_wg2[l]).astype(BF16),
        gla_bg=row(gla_bg[l]), gla_ng=row(jnp.tile(gla_norm_g[l], GLA_HEADS)),
        sb_bias=sb_bias[l],
        rw=[row(mu[:RW_W]), row(mu[RW_W:2 * RW_W]), row(mu[2 * RW_W:3 * RW_W]), row(mu[3 * RW_W:]),
            row(rw_w0[l]), lr_pad(rw_w2[l], 0), row(rw_a0[l]), lr_pad(rw_a2[l], RW_LR_W),
            lr_pad(rw_g2[l], RW_LR_W + RW_LR_A), row(rw_kk[l]), row(rw_ka[l]), row(rw_rk[l].reshape(-1)),
            row(rw_lnx_g[l]), row(rw_lnx_b[l])],
        mem_wq=mem_wq[l].astype(BF16), mem_wo=mem_wo[l].astype(BF16),
        mem_wkv=jnp.concatenate([mem_wk[l], mem_wv[l]], axis=1).astype(BF16),
    )


def _split_shift(sh):
    return [sh[:, None, :RW_W], sh[:, None, RW_W:2 * RW_W], sh[:, None, 2 * RW_W:3 * RW_W], sh[:, None, 3 * RW_W:]]


def _mixers_recurrent(proj, nb, t_len, pk, seg, gla_s0, rw_s0, rw_shift0, t_valid):
    oa, gla_s = _gla(proj, nb, t_len, pk["gla_wg2"], pk["gla_bg"], pk["gla_ng"],
                     _blockdiag_in(jnp.swapaxes(gla_s0, 2, 3), GLA_HEADS // 2), t_valid)
    oc, rw_s = _rwkv(proj, nb, t_len, _split_shift(rw_shift0), pk["rw"], seg,
                     _blockdiag_in(rw_s0, RW_HEADS // 2), t_valid)
    gla_s = jnp.swapaxes(_blockdiag_out(gla_s, GLA_DV, GLA_DK), 2, 3)
    rw_s = _blockdiag_out(rw_s, RW_N, RW_N)
    return oa, oc, gla_s, rw_s


def kernel(x_prompt, x_sample, mem_prompt, cache_sb_k, cache_sb_v, page_table, state_gla, state_rwkv,
           state_rwkv_shift, cache_mem_k, cache_mem_v, ln_g, ln_b, ffn_w1, ffn_w3, ffn_w2, w_in, b_gate,
           gla_wg2, gla_bg, gla_norm_g, sb_bias, rw_mu, rw_w0, rw_w2, rw_a0, rw_a2, rw_g2, rw_kk, rw_ka,
           rw_rk, rw_lnx_g, rw_lnx_b, w_br, w_o, mem_wq, mem_wk, mem_wv, mem_wo):
    bp, tp, d = x_prompt.shape
    db, ts, _ = x_sample.shape
    mlen = mem_prompt.shape[1]
    n_pool = cache_sb_k.shape[1]
    assert tp % CHUNK == 0 and ts <= CHUNK

    lane = jnp.arange(2 * LANES)
    seg = (lane[:, None] // RW_N == lane[None, :] // RW_N).astype(BF16)
    seg = jnp.concatenate([seg, seg], axis=0)
    tq_sb = min(SB_TQ, tp)
    sidx = jnp.arange(min(SB_SUB, tq_sb))
    ms = -(sidx[:, None] >= sidx[None, :]).astype(BF16)
    kidx = jnp.arange(PAGE_SIZE)
    ms_s = -(kidx[:, None] > kidx[None, :]).astype(BF16)
    head_lane = (jnp.arange(SB_W)[None, :] // SB_DH == jnp.arange(SB_HEADS)[:, None]).astype(F32)
    cache_kt = cache_sb_k.transpose(0, 1, 3, 4, 2)
    cache_vt = cache_sb_v.transpose(0, 1, 3, 4, 2)
    tpad = 1 << (ts - 1).bit_length()

    yp = x_prompt.reshape(bp * tp, d)
    ys = x_sample.reshape(db * ts, d)
    memp = mem_prompt.reshape(bp * mlen, d)
    outs = {k: [] for k in ("kvp", "gp", "rp", "shp", "mk", "mv", "ks", "vs", "gs", "rs", "shs")}
    for l in range(DEPTH):
        pk = _pack_layer(l, ln_g, ln_b, ffn_w1, ffn_w3, ffn_w2, w_in, b_gate, gla_wg2, gla_bg, gla_norm_g, sb_bias,
                         rw_mu, rw_w0, rw_w2, rw_a0, rw_a2, rw_g2, rw_kk, rw_ka, rw_rk, rw_lnx_g, rw_lnx_b, w_br,
                         w_o, mem_wq, mem_wk, mem_wv, mem_wo)
        mkv = _matmul(memp, pk["mem_wkv"])
        mk, mv = mkv[:, :d].reshape(bp, mlen, d), mkv[:, d:].reshape(bp, mlen, d)
        x1 = _ffn_ln(yp, *pk["ffn"][0], pk["ln_g"][0], pk["ln_b"][0])
        proj, kv_new = _matmul(x1, pk["w_in"], tap_block=COL_SB_K // (2 * SB_W))
        oa, oc, gla_s, rw_s = _mixers_recurrent(
            proj, bp, tp, pk, seg, jnp.zeros((bp, GLA_HEADS, GLA_DK, GLA_DV), F32),
            jnp.zeros((bp, RW_HEADS, RW_N, RW_N), F32), jnp.zeros((bp, RW_COLS), F32), None)
        bias_rows = jnp.broadcast_to(pk["sb_bias"].reshape(SB_HEADS // 2, 2, 1), (SB_HEADS // 2, 2, tq_sb))
        ob = _sb_prompt(proj, bp, tp, bias_rows, ms)
        x2 = _merge_ln(x1, oa, ob, oc, proj, pk["b_gate"], pk["w_br"], pk["w_o"], pk["ln_g"][1], pk["ln_b"][1])
        x3 = _memattn_ln(x2, pk["mem_wq"], pk["mem_wo"], mk, mv, pk["ln_g"][2], pk["ln_b"][2], tp)
        yp = _ffn_ln(x3, *pk["ffn"][1], pk["ln_g"][3], pk["ln_b"][3])
        proj3 = proj.reshape(bp, tp, P_COLS)
        outs["kvp"].append(kv_new)
        outs["gp"].append(gla_s)
        outs["rp"].append(rw_s)
        outs["shp"].append(proj3[:, -1, COL_RW:COL_RW + RW_COLS])
        outs["mk"].append(mk.reshape(bp, mlen, MEM_HEADS, d // MEM_HEADS))
        outs["mv"].append(mv.reshape(bp, mlen, MEM_HEADS, d // MEM_HEADS))
        x1 = _ffn_ln(ys, *pk["ffn"][0], pk["ln_g"][0], pk["ln_b"][0])
        proj = _matmul(x1, pk["w_in"])
        proj3 = proj.reshape(db, ts, P_COLS)
        proj_pad = jnp.pad(proj3, ((0, 0), (0, CHUNK - ts), (0, 0))).reshape(db * CHUNK, P_COLS)
        oa, oc, gla_s, rw_s = _mixers_recurrent(proj_pad, db, CHUNK, pk, seg, state_gla[l], state_rwkv[l],
                                                state_rwkv_shift[l], ts)
        oa = oa.reshape(db, CHUNK, GLA_V)[:, :ts].reshape(db * ts, GLA_V)
        oc = oc.reshape(db, CHUNK, RW_W)[:, :ts].reshape(db * ts, RW_W)
        qs = proj3[:, :, COL_SB_Q:COL_SB_Q + SB_W]
        ks_new = proj3[:, :, COL_SB_K:COL_SB_K + SB_W]
        vs_new = proj3[:, :, COL_SB_V:COL_SB_V + SB_W]
        qpad = jnp.pad(qs, ((0, 0), (0, tpad - ts), (0, 0)))
        qbd = (qpad[:, None, :, :] * head_lane[None, :, None, :]).reshape(db, SB_HEADS * tpad, SB_W)
        pad_page = ((0, 0), (0, 0), (0, PAGE_SIZE - ts))
        bias_col = jnp.repeat(pk["sb_bias"], tpad).reshape(SB_HEADS * tpad, 1)
        ob = _sb_sample(qbd, bias_col, jnp.pad(jnp.swapaxes(ks_new, 1, 2), pad_page),
                        jnp.pad(jnp.swapaxes(vs_new, 1, 2), pad_page), cache_kt, cache_vt, page_table, l, ms_s)
        ob = jnp.einsum("bhtgd,hg->btgd", ob.reshape(db, SB_HEADS, tpad, SB_HEADS, SB_DH)[:, :, :ts],
                        jnp.eye(SB_HEADS, dtype=F32)).reshape(db * ts, SB_W)
        x2 = _merge_ln(x1, oa, ob, oc, proj, pk["b_gate"], pk["w_br"], pk["w_o"], pk["ln_g"][1], pk["ln_b"][1])
        x3 = _memattn_ln(x2, pk["mem_wq"], pk["mem_wo"], cache_mem_k[l].reshape(db, mlen, d),
                         cache_mem_v[l].reshape(db, mlen, d), pk["ln_g"][2], pk["ln_b"][2], ts)
        ys = _ffn_ln(x3, *pk["ffn"][1], pk["ln_g"][3], pk["ln_b"][3])
        outs["ks"].append(ks_new.reshape(db, ts, SB_HEADS, SB_DH))
        outs["vs"].append(vs_new.reshape(db, ts, SB_HEADS, SB_DH))
        outs["gs"].append(gla_s)
        outs["rs"].append(rw_s)
        outs["shs"].append(proj3[:, -1, COL_RW:COL_RW + RW_COLS])
    st = lambda k: jnp.stack(outs[k])
    kvp = st("kvp")
    heads_p = lambda a: a.reshape(DEPTH, bp, tp, SB_HEADS, SB_DH)
    return (yp.reshape(bp, tp, d), ys.reshape(db, ts, d), heads_p(kvp[:, :, :SB_W]), heads_p(kvp[:, :, SB_W:]),
            st("gp"), st("rp"), st("shp"),
            st("mk"), st("mv"), st("ks"), st("vs"), st("gs"), st("rs"), st("shs"))
```

```python
import functools
import math

import jax
import jax.numpy as jnp
from jax import lax
from jax.experimental import pallas as pl
from jax.experimental.pallas import tpu as pltpu

F32 = jnp.float32
BF16 = jnp.bfloat16

DEPTH = 4
DN_ALPHA = (2 * DEPTH) ** 0.25
LN_EPS = 1e-5
LOG2E = 1.4426950408889634
FFN_RES = 0.5
GLA_HEADS, GLA_DK, GLA_DV, GLA_LR = 4, 64, 128, 16
GLA_GATE_NORM = 16.0
SB_HEADS, SB_DH = 8, 64
RW_HEADS, RW_N = 8, 64
RW_LR_W, RW_LR_A, RW_LR_G = 64, 64, 128
RW_LN_EPS = 64e-5
N_BRANCH = 3
MEM_HEADS = 4
PAGE_SIZE = 128

GLA_QK = GLA_HEADS * GLA_DK
GLA_V = GLA_HEADS * GLA_DV
SB_W = SB_HEADS * SB_DH
RW_W = RW_HEADS * RW_N
RW_LR = RW_LR_W + RW_LR_A + RW_LR_G
RW_COLS = 3 * RW_W + RW_LR

COL_GLA_Q, COL_GLA_K, COL_GLA_V, COL_GLA_G = 0, 256, 512, 1024
COL_SB_Q, COL_SB_K, COL_SB_V = 1536, 2048, 2560
COL_GATE = 3072
COL_RW = 6144
COL_RW_LR = COL_RW + 3 * RW_W
COL_GDA = COL_RW + RW_COLS
P_COLS = 8192

LANES = 128
CHUNK = 64
VMEM_LIMIT = 56 * 1024 * 1024


def _cparams(sem):
    return pltpu.CompilerParams(dimension_semantics=sem, vmem_limit_bytes=VMEM_LIMIT)


def _iota(shape, dim):
    return lax.broadcasted_iota(jnp.int32, shape, dim)


_DN = {"nn": (((1,), (0,)), ((), ())), "nt": (((1,), (1,)), ((), ())), "tn": (((0,), (0,)), ((), ()))}


def _pdot(a, b, kind="nn"):
    return lax.dot_general(a.astype(BF16), b.astype(BF16), _DN[kind], preferred_element_type=F32)


def _split3(x):
    h = x.astype(BF16)
    r = x - h.astype(F32)
    m = r.astype(BF16)
    lo = (r - m.astype(F32)).astype(BF16)
    return h, m, lo


def _exact_left(mat_bf16, x):
    h, m, lo = _split3(x)
    d = lambda y: jnp.dot(mat_bf16, y, preferred_element_type=F32)
    return d(h) + (d(m) + d(lo))


def _exact_right(x, mat_bf16, terms=3):
    h, m, lo = _split3(x)
    d = lambda y: jnp.dot(y, mat_bf16, preferred_element_type=F32)
    if terms == 2:
        return d(h) + d(m)
    return d(h) + (d(m) + d(lo))


def _seg_sum(x, seg2_bf16):
    w = seg2_bf16.shape[1]
    parts = []
    for j in range(x.shape[1] // w):
        xj = x[:, j * w:(j + 1) * w]
        h = xj.astype(BF16)
        m = (xj - h.astype(F32)).astype(BF16)
        parts.append(jnp.dot(jnp.concatenate([h, m], axis=1), seg2_bf16, preferred_element_type=F32))
    return jnp.concatenate(parts, axis=1)


def _layer_norm(y, g, b):
    mu = jnp.mean(y, axis=-1, keepdims=True)
    d = y - mu
    var = jnp.mean(d * d, axis=-1, keepdims=True)
    return d * lax.rsqrt(var + LN_EPS) * g + b


def _log_sigmoid(x):
    return jnp.minimum(x, 0.0) - jnp.log1p(jnp.exp(-jnp.abs(x)))


def _ffn_ln_body(x_ref, w1_ref, w3_ref, w2_ref, g_ref, b_ref, o_ref, xb_ref, acc_ref):
    j = pl.program_id(1)

    @pl.when(j == 0)
    def _():
        xb_ref[...] = x_ref[...].astype(BF16)
        acc_ref[...] = jnp.zeros_like(acc_ref)

    xb = xb_ref[...]
    h1 = jnp.dot(xb, w1_ref[...], preferred_element_type=F32)
    h3 = jnp.dot(xb, w3_ref[...], preferred_element_type=F32)
    h = (h1 * jax.nn.sigmoid(h1)) * h3
    acc_ref[...] += jnp.dot(h.astype(BF16), w2_ref[...], preferred_element_type=F32)

    @pl.when(j == pl.num_programs(1) - 1)
    def _():
        y = DN_ALPHA * x_ref[...] + FFN_RES * acc_ref[...]
        o_ref[...] = _layer_norm(y, g_ref[...], b_ref[...])


def _ffn_ln(x, w1, w3, w2, g, b):
    n, d = x.shape
    dff = w1.shape[1]
    tm = min(n, 512)
    tf = dff // 2 if (dff // 2) % LANES == 0 else 256
    assert n % tm == 0 and dff % tf == 0
    return pl.pallas_call(
        _ffn_ln_body,
        grid=(n // tm, dff // tf),
        in_specs=[
            pl.BlockSpec((tm, d), lambda i, j: (i, 0)),
            pl.BlockSpec((d, tf), lambda i, j: (0, j)),
            pl.BlockSpec((d, tf), lambda i, j: (0, j)),
            pl.BlockSpec((tf, d), lambda i, j: (j, 0)),
            pl.BlockSpec((1, d), lambda i, j: (0, 0)),
            pl.BlockSpec((1, d), lambda i, j: (0, 0)),
        ],
        out_specs=pl.BlockSpec((tm, d), lambda i, j: (i, 0)),
        out_shape=jax.ShapeDtypeStruct((n, d), F32),
        scratch_shapes=[pltpu.VMEM((tm, d), BF16), pltpu.VMEM((tm, d), F32)],
        compiler_params=_cparams(("parallel", "arbitrary")),
        name="ffn_ln",
    )(x, w1, w3, w2, g, b)


def _mm_body(x_ref, w_ref, *rest, with_t):
    if with_t:
        wt_ref, o_ref, ot_ref, xb_ref = rest
    else:
        o_ref, xb_ref = rest
    j = pl.program_id(1)

    @pl.when(j == 0)
    def _():
        xb_ref[...] = x_ref[...].astype(BF16)
        if with_t:
            ot_ref[0] = lax.dot_general(wt_ref[...], xb_ref[...], _DN["nt"], preferred_element_type=F32)

    o_ref[...] = jnp.dot(xb_ref[...], w_ref[...], preferred_element_type=F32)


def _matmul(x, w, wt=None, rows_per_batch=None):
    n, k = x.shape
    m = w.shape[1]
    tm = min(n, 1024) if wt is None else min(n, 1024, rows_per_batch)
    tn = min(m, 1024)
    assert n % tm == 0 and m % tn == 0
    in_specs = [pl.BlockSpec((tm, k), lambda i, j: (i, 0)), pl.BlockSpec((k, tn), lambda i, j: (0, j))]
    out_specs = [pl.BlockSpec((tm, tn), lambda i, j: (i, j))]
    out_shape = [jax.ShapeDtypeStruct((n, m), F32)]
    args = [x, w]
    if wt is not None:
        assert rows_per_batch % tm == 0
        tpb = rows_per_batch // tm
        in_specs.append(pl.BlockSpec(wt.shape, lambda i, j: (0, 0)))
        out_specs.append(pl.BlockSpec((1, wt.shape[0], tm), lambda i, j: (i // tpb, 0, i % tpb)))
        out_shape.append(jax.ShapeDtypeStruct((n // rows_per_batch, wt.shape[0], rows_per_batch), F32))
        args.append(wt)
    res = pl.pallas_call(
        functools.partial(_mm_body, with_t=wt is not None),
        grid=(n // tm, m // tn),
        in_specs=in_specs,
        out_specs=out_specs,
        out_shape=out_shape,
        scratch_shapes=[pltpu.VMEM((tm, k), BF16)],
        compiler_params=_cparams(("parallel", "arbitrary")),
        name="proj_matmul",
    )(*args)
    return res if wt is not None else res[0]


def _merge_body(x_ref, oa_ref, ob_ref, oc_ref, pg0_ref, pg1_ref, pg2_ref, bg_ref, wbr_ref, wo_ref,
                g_ref, b_ref, o_ref):
    s = None
    for n, (o_r, pg_r) in enumerate(((oa_ref, pg0_ref), (ob_ref, pg1_ref), (oc_ref, pg2_ref))):
        br = jnp.dot(o_r[...].astype(BF16), wbr_ref[n], preferred_element_type=F32)
        term = jax.nn.sigmoid(pg_r[...] + bg_ref[n]) * br
        s = term if s is None else s + term
    mix = jnp.dot(s.astype(BF16), wo_ref[...], preferred_element_type=F32)
    o_ref[...] = _layer_norm(DN_ALPHA * x_ref[...] + mix, g_ref[...], b_ref[...])


def _merge_ln(x, oa, ob, oc, proj, b_gate, w_br, w_o, g, b):
    n, d = x.shape
    bw = oa.shape[1]
    tm = min(n, 512)
    gate_blk = COL_GATE // d
    row = lambda i: (i, 0)
    return pl.pallas_call(
        _merge_body,
        grid=(n // tm,),
        in_specs=[
            pl.BlockSpec((tm, d), row),
            pl.BlockSpec((tm, bw), row),
            pl.BlockSpec((tm, bw), row),
            pl.BlockSpec((tm, bw), row),
            pl.BlockSpec((tm, d), lambda i: (i, gate_blk)),
            pl.BlockSpec((tm, d), lambda i: (i, gate_blk + 1)),
            pl.BlockSpec((tm, d), lambda i: (i, gate_blk + 2)),
            pl.BlockSpec((N_BRANCH, 1, d), lambda i: (0, 0, 0)),
            pl.BlockSpec((N_BRANCH, bw, d), lambda i: (0, 0, 0)),
            pl.BlockSpec((d, d), lambda i: (0, 0)),
            pl.BlockSpec((1, d), lambda i: (0, 0)),
            pl.BlockSpec((1, d), lambda i: (0, 0)),
        ],
        out_specs=pl.BlockSpec((tm, d), row),
        out_shape=jax.ShapeDtypeStruct((n, d), F32),
        compiler_params=_cparams(("parallel",)),
        name="merge_ln",
    )(x, oa, ob, oc, proj, proj, proj, b_gate, w_br, w_o, g, b)


def _memattn_body(x_ref, wq_ref, wo_ref, mk_ref, mv_ref, g_ref, b_ref, o_ref, q_ref, acc_ref, *,
                  t_seq, tiles_per_batch, masked):
    i = pl.program_id(0)
    j = pl.program_id(1)
    tm, d = x_ref.shape
    dh = d // MEM_HEADS

    @pl.when(j == 0)
    def _():
        q_ref[...] = jnp.dot(x_ref[...].astype(BF16), wq_ref[...], preferred_element_type=F32).astype(BF16)
        acc_ref[...] = jnp.zeros_like(acc_ref)

    if masked:
        row = i * tm + _iota((tm, 1), 0)
        mem_b = i // tiles_per_batch + j
        keep = (row >= mem_b * t_seq) & (row < (mem_b + 1) * t_seq)
    sls = [slice(h * dh, (h + 1) * dh) for h in range(MEM_HEADS)]
    ss = [lax.dot_general(q_ref[:, sl], mk_ref[0, :, sl].astype(BF16), _DN["nt"], preferred_element_type=F32)
          * (dh ** -0.5) for sl in sls]
    es = [jnp.exp(s - jnp.max(s, axis=-1, keepdims=True)) for s in ss]
    atts = [e / jnp.sum(e, axis=-1, keepdims=True) for e in es]
    ohs = [jnp.dot(att.astype(BF16), mv_ref[0, :, sl].astype(BF16), preferred_element_type=F32)
           for att, sl in zip(atts, sls)]
    for oh, sl in zip(ohs, sls):
        if masked:
            oh = jnp.where(keep, oh, 0.0)
        acc_ref[:, sl] += oh

    @pl.when(j == pl.num_programs(1) - 1)
    def _():
        xm = jnp.dot(acc_ref[...].astype(BF16), wo_ref[...], preferred_element_type=F32)
        o_ref[...] = _layer_norm(DN_ALPHA * x_ref[...] + xm, g_ref[...], b_ref[...])


def _memattn_ln(x, wq, wo, mem_k, mem_v, g, b, t_seq):
    n, d = x.shape
    nb, m, _ = mem_k.shape
    tm = min(n, 512)
    if t_seq >= tm:
        assert t_seq % tm == 0
        tiles_per_batch, nj, masked = t_seq // tm, 1, False
    else:
        assert tm % t_seq == 0 and n == tm
        tiles_per_batch, nj, masked = 1, nb, True
    body = functools.partial(_memattn_body, t_seq=t_seq, tiles_per_batch=tiles_per_batch, masked=masked)
    mem_map = lambda i, j: (i // tiles_per_batch + j, 0, 0)
    return pl.pallas_call(
        body,
        grid=(n // tm, nj),
        in_specs=[
            pl.BlockSpec((tm, d), lambda i, j: (i, 0)),
            pl.BlockSpec((d, d), lambda i, j: (0, 0)),
            pl.BlockSpec((d, d), lambda i, j: (0, 0)),
            pl.BlockSpec((1, m, d), mem_map),
            pl.BlockSpec((1, m, d), mem_map),
            pl.BlockSpec((1, d), lambda i, j: (0, 0)),
            pl.BlockSpec((1, d), lambda i, j: (0, 0)),
        ],
        out_specs=pl.BlockSpec((tm, d), lambda i, j: (i, 0)),
        out_shape=jax.ShapeDtypeStruct((n, d), F32),
        scratch_shapes=[pltpu.VMEM((tm, d), BF16), pltpu.VMEM((tm, d), F32)],
        compiler_params=_cparams(("parallel", "arbitrary")),
        name="memattn_ln",
    )(x, wq, wo, mem_k, mem_v, g, b)


def _stack2(x):
    return jnp.concatenate([x, x], axis=0)


def _head_rows_mask(c, width, seg):
    r = _iota((2 * c, width), 0)
    l = _iota((2 * c, width), 1)
    return ((r < c) & (l < seg)) | ((r >= c) & (l >= seg))


GLA_CHUNKS_PER_ITER = 4


def _gla_body(q_ref, k_ref, v_ref, g_ref, gda_ref, wg2_ref, bg_ref, ng_ref, s0_ref, o_ref, sout_ref, s_ref, *,
              t_valid):
    t = pl.program_id(1)
    tt = q_ref.shape[0]
    c = CHUNK
    npair = GLA_HEADS // 2
    kw, vw = 2 * GLA_DK, 2 * GLA_DV

    @pl.when(t == 0)
    def _():
        s_ref[...] = s0_ref[0]

    ri = _iota((c, c), 0)
    ci = _iota((c, c), 1)
    tri_incl = (ci <= ri).astype(F32).astype(BF16)
    r2 = _iota((2 * c, 2 * c), 0)
    c2 = _iota((2 * c, 2 * c), 1)
    same_head = ((r2 < c) & (c2 < c)) | ((r2 >= c) & (c2 >= c))
    att_mask = same_head & ((c2 & (c - 1)) <= (r2 & (c - 1)))
    qmask = _head_rows_mask(c, kw, GLA_DK)
    omask = _head_rows_mask(c, vw, GLA_DV)
    sr = _iota((vw, kw), 0)
    sc = _iota((vw, kw), 1)
    bd_mask = ((sr < GLA_DV) & (sc < GLA_DK)) | ((sr >= GLA_DV) & (sc >= GLA_DK))

    cpi = min(GLA_CHUNKS_PER_ITER, tt // c)
    pairs = range(npair)
    kls = [slice(p * kw, (p + 1) * kw) for p in pairs]
    vls = [slice(p * vw, (p + 1) * vw) for p in pairs]

    def chunk_group(ig, carry):
        units = [(j, p) for j in range(cpi) for p in pairs]
        grows = pl.ds(pl.multiple_of(ig * (cpi * c), cpi * c), cpi * c)
        rows = [pl.ds(pl.multiple_of((ig * cpi + j) * c, c), c) for j in range(cpi)]
        la = _log_sigmoid(_pdot(gda_ref[grows, :], wg2_ref[...]) + bg_ref[...]) / GLA_GATE_NORM
        kk = k_ref[grows, :]
        if t_valid is not None:
            valid = (t * tt + ig * (cpi * c) + _iota((cpi * c, 1), 0)) < t_valid
            la = jnp.where(valid, la, 0.0)
            kk = jnp.where(valid, kk, 0.0)
        la3 = _split3(la)
        bcs = [sum(jnp.dot(tri_incl, part[j * c:(j + 1) * c], preferred_element_type=F32) for part in la3)
               for j in range(cpi)]
        b_last = [bc[c - 1:c, :] for bc in bcs]
        e_last = [jnp.exp(b) for b in b_last]
        qs = q_ref[grows, :] * (GLA_DK ** -0.5)
        q_dec = [qs[j * c:(j + 1) * c] * jnp.exp(bcs[j]) for j in range(cpi)]
        k_inv = [kk[j * c:(j + 1) * c] * jnp.exp(-bcs[j]) for j in range(cpi)]
        k_end = [kk[j * c:(j + 1) * c] * jnp.exp(b_last[j] - bcs[j]) for j in range(cpi)]
        vv = [v_ref[rows[j], :] for j in range(cpi)]
        qd2 = {(j, p): jnp.where(qmask, _stack2(q_dec[j][:, kls[p]]), 0.0).astype(BF16) for j, p in units}
        att = {(j, p): jnp.where(att_mask, _pdot(qd2[j, p], _stack2(k_inv[j][:, kls[p]]), "nt"), 0.0)
               for j, p in units}
        intra = {(j, p): jnp.where(omask, _pdot(att[j, p], _stack2(vv[j][:, vls[p]])), 0.0) for j, p in units}
        s_cur = [s_ref[p] for p in pairs]
        for j in range(cpi):
            o2 = [_pdot(qd2[j, p], s_cur[p], "nt") + intra[j, p] for p in pairs]
            upd = [_pdot(vv[j][:, vls[p]], k_end[j][:, kls[p]], "tn") for p in pairs]
            s_cur = [s_cur[p] * e_last[j][:, kls[p]] + jnp.where(bd_mask, upd[p], 0.0) for p in pairs]
            normed = []
            for p in pairs:
                op = o2[p][:c] + o2[p][c:]
                for h in range(2):
                    oh = op[:, h * GLA_DV:(h + 1) * GLA_DV]
                    normed.append(oh * lax.rsqrt(jnp.mean(oh * oh, axis=-1, keepdims=True) + LN_EPS))
            o = jnp.concatenate(normed, axis=1) * ng_ref[...]
            gg = g_ref[rows[j], :]
            o_ref[rows[j], :] = o * (gg * jax.nn.sigmoid(gg))
        for p in pairs:
            s_ref[p] = s_cur[p]
        return carry

    lax.fori_loop(0, tt // (c * cpi), chunk_group, 0)

    @pl.when(t == pl.num_programs(1) - 1)
    def _():
        sout_ref[0] = s_ref[...]


def _gla(proj, nb, t_len, wg2p, bg, ng, s0bd, t_valid):
    tt = min(t_len, 512)
    nt = t_len // tt
    npair = GLA_HEADS // 2
    kw, vw = 2 * GLA_DK, 2 * GLA_DV
    body = functools.partial(_gla_body, t_valid=t_valid)

    def col(width, off):
        blk = off // width
        return pl.BlockSpec((tt, width), lambda b, t: (b * nt + t, blk))

    const2 = lambda b, t: (0, 0)
    return pl.pallas_call(
        body,
        grid=(nb, nt),
        in_specs=[
            col(GLA_QK, COL_GLA_Q), col(GLA_QK, COL_GLA_K), col(GLA_V, COL_GLA_V), col(GLA_V, COL_GLA_G),
            col(LANES, COL_GDA),
            pl.BlockSpec((LANES, GLA_QK), const2),
            pl.BlockSpec((1, GLA_QK), const2),
            pl.BlockSpec((1, GLA_V), const2),
            pl.BlockSpec((1, npair, vw, kw), lambda b, t: (b, 0, 0, 0)),
        ],
        out_specs=[
            pl.BlockSpec((tt, GLA_V), lambda b, t: (b * nt + t, 0)),
            pl.BlockSpec((1, npair, vw, kw), lambda b, t: (b, 0, 0, 0)),
        ],
        out_shape=[
            jax.ShapeDtypeStruct((nb * t_len, GLA_V), F32),
            jax.ShapeDtypeStruct((nb, npair, vw, kw), F32),
        ],
        scratch_shapes=[pltpu.VMEM((npair, vw, kw), F32)],
        compiler_params=_cparams(("parallel", "arbitrary")),
        name="gla",
    )(proj, proj, proj, proj, proj, wg2p, bg, ng, s0bd)


RW_CHUNKS_PER_ITER = 4


def _rwkv_body(r_ref, k_ref, v_ref, lr_ref, shr_ref, shk_ref, shv_ref, shlr_ref, mur_ref, muk_ref, muv_ref,
               mulr_ref, w0_ref, w2_ref, a0_ref, a2_ref, g2_ref, kkp_ref, ka_ref, rk_ref, lng_ref, lnb_ref,
               seg_ref, s0_ref, o_ref, sout_ref,
               s_ref, cr_ref, ck_ref, cv_ref, clr_ref, rs_ref, ws_ref, ks_ref, vs_ref, as_ref, bs_ref, ys_ref, *,
               t_valid):
    t = pl.program_id(1)
    tt = r_ref.shape[0]
    c = CHUNK
    npair = RW_HEADS // 2
    pw = 2 * RW_N

    @pl.when(t == 0)
    def _():
        s_ref[...] = s0_ref[0]
        cr_ref[...] = shr_ref[0]
        ck_ref[...] = shk_ref[0]
        cv_ref[...] = shv_ref[0]
        clr_ref[...] = shlr_ref[0]

    def lerp(x_ref, carry_ref, mu_ref):
        x = x_ref[...]
        prev = pltpu.roll(x, 1, 0)
        prev = jnp.where(_iota(x.shape, 0) == 0, carry_ref[...], prev)
        carry_ref[...] = x_ref[pl.ds(tt - 1, 1), :]
        return x + (prev - x) * mu_ref[...]

    xr = lerp(r_ref, cr_ref, mur_ref)
    xk = lerp(k_ref, ck_ref, muk_ref)
    xv = lerp(v_ref, cv_ref, muv_ref)
    xlr = lerp(lr_ref, clr_ref, mulr_ref)
    w_log = _log_sigmoid(w0_ref[...] + _pdot(jnp.tanh(xlr), w2_ref[...])) - 0.5
    wdec = -jnp.exp(w_log)
    a = jax.nn.sigmoid(a0_ref[...] + _pdot(xlr, a2_ref[...]))
    g = _pdot(jax.nn.sigmoid(xlr), g2_ref[...])
    kk = xk * kkp_ref[...]
    kkn = kk / jnp.maximum(jnp.sqrt(_seg_sum(kk * kk, seg_ref[...])), 1e-12)
    kmod = xk * (1.0 + (a - 1.0) * ka_ref[...])
    av = -kkn
    bv = kkn * a
    if t_valid is not None:
        valid = (t * tt + _iota((tt, 1), 0)) < t_valid
        wdec = jnp.where(valid, wdec, 0.0)
        av = jnp.where(valid, av, 0.0)
        bv = jnp.where(valid, bv, 0.0)
        kmod = jnp.where(valid, kmod, 0.0)
        xv = jnp.where(valid, xv, 0.0)
    rs_ref[...] = xr
    ws_ref[...] = wdec
    ks_ref[...] = kmod
    vs_ref[...] = xv
    as_ref[...] = av
    bs_ref[...] = bv

    ri = _iota((c, c), 0)
    ci = _iota((c, c), 1)
    tri_incl = (ci <= ri).astype(F32).astype(BF16)
    r2 = _iota((2 * c, 2 * c), 0)
    c2 = _iota((2 * c, 2 * c), 1)
    same_head = ((r2 < c) & (c2 < c)) | ((r2 >= c) & (c2 >= c))
    strict = same_head & ((c2 & (c - 1)) < (r2 & (c - 1)))
    incl = same_head & ((c2 & (c - 1)) <= (r2 & (c - 1)))
    eye = (r2 == c2).astype(F32)
    hmask = _head_rows_mask(c, pw, RW_N)
    bd_mask = _head_rows_mask(RW_N, pw, RW_N)

    cpi = min(RW_CHUNKS_PER_ITER, tt // c)
    pairs = range(npair)
    lanes = [slice(p * pw, (p + 1) * pw) for p in pairs]
    units = [(j, p) for j in range(cpi) for p in pairs]

    def chunk_group(ig, carry):
        rows, b_e, k_e, vv, e_last = [], [], [], [], []
        ar4, bk4, v2 = {}, {}, {}
        for j in range(cpi):
            rows.append(pl.ds(pl.multiple_of((ig * cpi + j) * c, c), c))
            w = ws_ref[rows[j], :]
            cs = _exact_left(tri_incl, w)
            c_last = cs[c - 1:c, :]
            e_neg = jnp.exp(-cs)
            e_end = jnp.exp(c_last - cs)
            e_last.append(jnp.exp(c_last))
            a_t = as_ref[rows[j], :] * jnp.exp(cs - w)
            r_t = rs_ref[rows[j], :] * jnp.exp(cs)
            bb = bs_ref[rows[j], :]
            kc = ks_ref[rows[j], :]
            b_t = bb * e_neg
            k_t = kc * e_neg
            b_e.append(bb * e_end)
            k_e.append(kc * e_end)
            vv.append(vs_ref[rows[j], :])
            for p in pairs:
                ln = lanes[p]
                ar4[j, p] = jnp.concatenate([jnp.where(hmask, _stack2(a_t[:, ln]), 0.0),
                                             jnp.where(hmask, _stack2(r_t[:, ln]), 0.0)], axis=0).astype(BF16)
                bk4[j, p] = jnp.concatenate([_stack2(b_t[:, ln]), _stack2(k_t[:, ln])], axis=0).astype(BF16)
                v2[j, p] = _stack2(vv[j][:, ln]).astype(BF16)
        g4 = {u: _pdot(ar4[u], bk4[u], "nt") for u in units}
        n_ab = {u: jnp.where(strict, g4[u][:2 * c, :2 * c], 0.0) for u in units}
        n_ak = {u: jnp.where(strict, g4[u][:2 * c, 2 * c:], 0.0) for u in units}
        n_r = {u: jnp.concatenate([jnp.where(incl, g4[u][2 * c:, :2 * c], 0.0),
                                   jnp.where(incl, g4[u][2 * c:, 2 * c:], 0.0)], axis=1).astype(BF16)
               for u in units}
        nakv = {u: jnp.where(hmask, _pdot(n_ak[u], v2[u]), 0.0) for u in units}
        tinv = {u: eye + n_ab[u] for u in units}
        x = n_ab
        for _ in range(int(math.log2(c)) - 1):
            x = {u: _pdot(x[u], x[u]) for u in units}
            tinv = {u: tinv[u] + _pdot(tinv[u], x[u]) for u in units}
        s_cur = [s_ref[p] for p in pairs]
        for j in range(cpi):
            as4 = [_pdot(ar4[j, p], s_cur[p], "nt") for p in pairs]
            u2 = [_pdot(tinv[j, p], as4[p][:2 * c] + nakv[j, p]) for p in pairs]
            y2 = [as4[p][2 * c:] + jnp.where(
                hmask, _pdot(n_r[j, p], jnp.concatenate([u2[p].astype(BF16), v2[j, p]], axis=0)), 0.0)
                  for p in pairs]
            upd = [_pdot(jnp.concatenate([u2[p][:c] + u2[p][c:], vv[j][:, lanes[p]]], axis=0),
                         jnp.concatenate([b_e[j][:, lanes[p]], k_e[j][:, lanes[p]]], axis=0), "tn") for p in pairs]
            s_cur = [s_cur[p] * e_last[j][:, lanes[p]] + jnp.where(bd_mask, upd[p], 0.0) for p in pairs]
            ys_ref[rows[j], :] = jnp.concatenate([y2[p][:c] + y2[p][c:] for p in pairs], axis=1)
        for p in pairs:
            s_ref[p] = s_cur[p]
        return carry

    lax.fori_loop(0, tt // (c * cpi), chunk_group, 0)

    y = ys_ref[...]
    seg = seg_ref[...]
    inv_n = 1.0 / RW_N
    mu = _seg_sum(y, seg) * inv_n
    d = y - mu
    var = _seg_sum(d * d, seg) * inv_n
    yn = d * lax.rsqrt(var + RW_LN_EPS) * lng_ref[...] + lnb_ref[...]
    bonus = _seg_sum(xr * kmod * rk_ref[...], seg)
    o_ref[...] = (yn + bonus * xv) * g

    @pl.when(t == pl.num_programs(1) - 1)
    def _():
        sout_ref[0] = s_ref[...]


def _rwkv(proj, nb, t_len, shifts, prm, seg, s0bd, t_valid):
    tt = min(t_len, 256)
    nt = t_len // tt
    npair = RW_HEADS // 2
    pw = 2 * RW_N
    body = functools.partial(_rwkv_body, t_valid=t_valid)

    def col(width, off):
        blk = off // width
        return pl.BlockSpec((tt, width), lambda b, t: (b * nt + t, blk))

    def per_batch(width):
        return pl.BlockSpec((1, 1, width), lambda b, t: (b, 0, 0))

    def const(shape):
        return pl.BlockSpec(shape, lambda b, t: (0,) * len(shape))

    vec = const((1, RW_W))
    lrm = const((RW_LR, RW_W))
    state = pl.BlockSpec((1, npair, pw, pw), lambda b, t: (b, 0, 0, 0))
    tile = lambda: pltpu.VMEM((tt, RW_W), F32)
    return pl.pallas_call(
        body,
        grid=(nb, nt),
        in_specs=[
            col(RW_W, COL_RW), col(RW_W, COL_RW + RW_W), col(RW_W, COL_RW + 2 * RW_W), col(RW_LR, COL_RW_LR),
            per_batch(RW_W), per_batch(RW_W), per_batch(RW_W), per_batch(RW_LR),
            vec, vec, vec, const((1, RW_LR)),
            vec, lrm, vec, lrm, lrm, vec, vec, vec, vec, vec,
            const(seg.shape),
            state,
        ],
        out_specs=[pl.BlockSpec((tt, RW_W), lambda b, t: (b * nt + t, 0)), state],
        out_shape=[
            jax.ShapeDtypeStruct((nb * t_len, RW_W), F32),
            jax.ShapeDtypeStruct((nb, npair, pw, pw), F32),
        ],
        scratch_shapes=[
            pltpu.VMEM((npair, pw, pw), F32),
            pltpu.VMEM((1, RW_W), F32), pltpu.VMEM((1, RW_W), F32), pltpu.VMEM((1, RW_W), F32),
            pltpu.VMEM((1, RW_LR), F32),
            tile(), tile(), tile(), tile(), tile(), tile(), tile(),
        ],
        compiler_params=_cparams(("parallel", "arbitrary")),
        name="rwkv7",
    )(proj, proj, proj, proj, *shifts, *prm, seg, s0bd)


SB_TQ = 512
SB_SUB = 256


def _sbp_body(q_ref, k_ref, v_ref, bias_ref, ms_ref, o_ref):
    i = pl.program_id(2)
    tq = q_ref.shape[0]
    tk = tq
    sub = min(SB_SUB, tk)
    nsub = tk // sub
    ms = ms_ref[...]
    lane = _iota((1, LANES), 1)
    q = q_ref[...] * (SB_DH ** -0.5 * LOG2E)
    causal = _iota((tq, tk), 1) < _iota((tq, tk), 0)
    heads = range(2)
    hms = [(lane >= h * SB_DH) & (lane < (h + 1) * SB_DH) for h in heads]
    qhs = [jnp.where(hms[h], q, 0.0).astype(BF16) for h in heads]
    biases = [bias_ref[0, h:h + 1, :] * LOG2E for h in heads]
    units = [(h, slice(j * sub, (j + 1) * sub)) for h in heads for j in reversed(range(nsub))]

    def block(kb, carry, masked):
        accs, laters = list(carry[0]), list(carry[1])
        rows = pl.ds(pl.multiple_of(kb * tk, tk), tk)
        kblk = k_ref[rows, :].astype(BF16)
        vblk = v_ref[rows, :].astype(BF16)
        z = [lax.dot_general(qhs[h], kblk, _DN["nt"], preferred_element_type=F32) + biases[h] for h in heads]
        sp = [jnp.maximum(zh, 0.0) + jnp.log2(1.0 + jnp.exp2(-jnp.abs(zh))) for zh in z]
        if masked:
            sp = [jnp.where(causal, s, 0.0) for s in sp]
        ws = [jnp.exp2(z[h][:, sl] + jnp.dot(sp[h][:, sl].astype(BF16), ms, preferred_element_type=F32))
              for h, sl in units]
        if masked:
            ws = [jnp.where(causal[:, sl], w, 0.0) for (h, sl), w in zip(units, ws)]
        pvs = [jnp.dot(w.astype(BF16), vblk[sl], preferred_element_type=F32) for (h, sl), w in zip(units, ws)]
        tots = [jnp.sum(sp[h][:, sl], axis=1, keepdims=True) for h, sl in units]
        for (h, sl), pv, tot in zip(units, pvs, tots):
            accs[h] = accs[h] + jnp.exp2(laters[h]) * pv
            laters[h] = laters[h] - tot
        return tuple(accs), tuple(laters)

    zero = (tuple(jnp.zeros((tq, LANES), F32) for _ in heads), tuple(jnp.zeros((tq, 1), F32) for _ in heads))
    accs, _ = lax.fori_loop(0, i, lambda n, cr: block(i - 1 - n, cr, False), block(i, zero, True))
    o_ref[...] = sum(jnp.where(hms[h], accs[h], 0.0) for h in heads)


def _sb_prompt(proj, nb, t_len, bias_rows, ms):
    tq = min(SB_TQ, t_len)
    nq = t_len // tq
    npair = SB_HEADS // 2
    qb, kb, vb = COL_SB_Q // LANES, COL_SB_K // LANES, COL_SB_V // LANES
    return pl.pallas_call(
        _sbp_body,
        grid=(nb, npair, nq),
        in_specs=[
            pl.BlockSpec((tq, LANES), lambda b, p, i: (b * nq + i, qb + p)),
            pl.BlockSpec((t_len, LANES), lambda b, p, i: (b, kb + p)),
            pl.BlockSpec((t_len, LANES), lambda b, p, i: (b, vb + p)),
            pl.BlockSpec((1, 2, tq), lambda b, p, i: (p, 0, 0)),
            pl.BlockSpec(ms.shape, lambda b, p, i: (0, 0)),
        ],
        out_specs=pl.BlockSpec((tq, LANES), lambda b, p, i: (b * nq + i, p)),
        out_shape=jax.ShapeDtypeStruct((nb * t_len, SB_W), F32),
        compiler_params=_cparams(("parallel", "parallel", "arbitrary")),
        name="sb_prompt",
    )(proj, proj, proj, bias_rows, ms)


SB_PAGES_PER_STEP = 16


def _sbs_body(pt_ref, q_ref, bias_ref, knew_ref, vnew_ref, *rest):
    g_pages = SB_PAGES_PER_STEP
    k_refs = rest[:g_pages]
    v_refs = rest[g_pages:2 * g_pages]
    ms_ref, o_ref, acc_ref, later_ref = rest[2 * g_pages:]
    s = pl.program_id(1)
    nrow, width = q_ref.shape[1:]
    nkey = knew_ref.shape[2]
    tpad = nrow // SB_HEADS
    q2 = (q_ref[0] * (SB_DH ** -0.5)).astype(BF16)
    bias = bias_ref[...]
    ms = ms_ref[...]
    fresh_ok = _iota((nrow, nkey), 1) < (_iota((nrow, nkey), 0) & (tpad - 1))

    def page_terms(kt, keep):
        z = jnp.dot(q2, kt.astype(BF16), preferred_element_type=F32) + bias
        sp = jnp.maximum(z, 0.0) + jnp.log1p(jnp.exp(-jnp.abs(z)))
        zs = z - sp
        if keep is not None:
            sp = jnp.where(keep, sp, 0.0)
        return zs, _exact_right(sp, ms, terms=2), jnp.sum(sp, axis=1, keepdims=True)

    def combine(pages, keep, carry):
        acc, later = carry
        for (zs, cum, tot), vt in pages:
            w = jnp.exp(zs + (cum + later))
            later = later - tot
            if keep is not None:
                w = jnp.where(keep, w, 0.0)
            acc = acc + lax.dot_general(w.astype(BF16), vt.astype(BF16), _DN["nt"], preferred_element_type=F32)
        return acc, later

    @pl.when(s == 0)
    def _():
        zero = (jnp.zeros(acc_ref.shape, F32), jnp.zeros(later_ref.shape, F32))
        acc_ref[...], later_ref[...] = combine([(page_terms(knew_ref[0], fresh_ok), vnew_ref[0])], fresh_ok, zero)

    @pl.when(s > 0)
    def _():
        flat = lambda ref: ref[0, 0].reshape(width, nkey)
        pages = [(page_terms(flat(k_refs[g]), None), flat(v_refs[g])) for g in range(g_pages)]
        acc_ref[...], later_ref[...] = combine(pages, None, (acc_ref[...], later_ref[...]))

    @pl.when(s == pl.num_programs(1) - 1)
    def _():
        o_ref[0] = acc_ref[...]


def _sb_sample(qbd, bias_col, knew_t, vnew_t, cache_kt, cache_vt, page_table, layer, ms):
    nb, nrow, width = qbd.shape
    nkey = knew_t.shape[2]
    n_pages = page_table.shape[1]
    g_pages = SB_PAGES_PER_STEP
    assert n_pages % g_pages == 0
    nsteps = n_pages // g_pages

    def page_spec(g):
        def imap(b, s, pt):
            page = n_pages - 1 - (jnp.maximum(s, 1) - 1) * g_pages - g
            return (layer, pt[b, page], 0, 0, 0)
        return pl.BlockSpec((1, 1) + cache_kt.shape[2:], imap)

    per_b = lambda b, s, pt: (b, 0, 0)
    grid_spec = pltpu.PrefetchScalarGridSpec(
        num_scalar_prefetch=1,
        grid=(nb, nsteps + 1),
        in_specs=[
            pl.BlockSpec((1, nrow, width), per_b),
            pl.BlockSpec((nrow, 1), lambda b, s, pt: (0, 0)),
            pl.BlockSpec((1, width, nkey), per_b),
            pl.BlockSpec((1, width, nkey), per_b),
            *[page_spec(g) for g in range(g_pages)],
            *[page_spec(g) for g in range(g_pages)],
            pl.BlockSpec(ms.shape, lambda b, s, pt: (0, 0)),
        ],
        out_specs=pl.BlockSpec((1, nrow, width), per_b),
        scratch_shapes=[pltpu.VMEM((nrow, width), F32), pltpu.VMEM((nrow, 1), F32)],
    )
    return pl.pallas_call(
        _sbs_body,
        grid_spec=grid_spec,
        out_shape=jax.ShapeDtypeStruct((nb, nrow, width), F32),
        compiler_params=_cparams(("parallel", "arbitrary")),
        name="sb_sample",
    )(page_table, qbd, bias_col, knew_t, vnew_t, *([cache_kt] * g_pages), *([cache_vt] * g_pages), ms)


def _blockdiag_in(s, pairs):
    b, h, do, di = s.shape
    s = s.reshape(b, pairs, 2, do, di)
    return jnp.einsum("bphvk,hg->bphvgk", s, jnp.eye(2, dtype=s.dtype)).reshape(b, pairs, 2 * do, 2 * di)


def _blockdiag_out(sbd, do, di):
    b, pairs = sbd.shape[:2]
    x = sbd.reshape(b, pairs, 2, do, 2, di)
    return jnp.einsum("bphvhk->bphvk", x).reshape(b, 2 * pairs, do, di)


def _pack_layer(l, ln_g, ln_b, ffn_w1, ffn_w3, ffn_w2, w_in, b_gate, gla_wg2, gla_bg, gla_norm_g, sb_bias, rw_mu,
                rw_w0, rw_w2, rw_a0, rw_a2, rw_g2, rw_kk, rw_ka, rw_rk, rw_lnx_g, rw_lnx_b, w_br, w_o, mem_wq,
                mem_wk, mem_wv, mem_wo):
    d = w_in.shape[1]
    gla_cols = 2 * GLA_QK + 2 * GLA_V + GLA_LR
    o_sb = gla_cols
    o_rw = o_sb + 3 * SB_W
    o_gate = o_rw + RW_COLS
    w = w_in[l]
    w_re = jnp.concatenate([
        w[:, :gla_cols - GLA_LR], w[:, o_sb:o_rw], w[:, o_gate:], w[:, o_rw:o_gate],
        w[:, gla_cols - GLA_LR:gla_cols], jnp.zeros((d, P_COLS - COL_GDA - GLA_LR), F32)], axis=1)
    assert w_re.shape[1] == P_COLS
    row = lambda v: v.reshape(1, -1)
    mu = rw_mu[l]

    def lr_pad(m, off):
        return jnp.zeros((RW_LR, RW_W), F32).at[off:off + m.shape[0]].set(m).astype(BF16)

    return dict(
        ln_g=[row(ln_g[l, i]) for i in range(4)], ln_b=[row(ln_b[l, i]) for i in range(4)],
        ffn=[(ffn_w1[l, i].astype(BF16), ffn_w3[l, i].astype(BF16), ffn_w2[l, i].astype(BF16)) for i in range(2)],
        w_in=w_re.astype(BF16),
        w_kv_t=w_re[:, COL_SB_K:COL_SB_K + 2 * SB_W].T.astype(BF16),
        b_gate=b_gate[l].reshape(N_BRANCH, 1, -1), w_br=w_br[l].astype(BF16), w_o=w_o[l].astype(BF16),
        gla_wg2=jnp.zeros((LANES, GLA_QK), F32).at[:GLA_LR].set(gla_wg2[l]).astype(BF16),
        gla_bg=row(gla_bg[l]), gla_ng=row(jnp.tile(gla_norm_g[l], GLA_HEADS)),
        sb_bias=sb_bias[l],
        rw=[row(mu[:RW_W]), row(mu[RW_W:2 * RW_W]), row(mu[2 * RW_W:3 * RW_W]), row(mu[3 * RW_W:]),
            row(rw_w0[l]), lr_pad(rw_w2[l], 0), row(rw_a0[l]), lr_pad(rw_a2[l], RW_LR_W),
            lr_pad(rw_g2[l], RW_LR_W + RW_LR_A), row(rw_kk[l]), row(rw_ka[l]), row(rw_rk[l].reshape(-1)),
            row(rw_lnx_g[l]), row(rw_lnx_b[l])],
        mem_wq=mem_wq[l].astype(BF16), mem_wo=mem_wo[l].astype(BF16),
        mem_wkv=jnp.concatenate([mem_wk[l], mem_wv[l]], axis=1).astype(BF16),
    )


def _split_shift(sh):
    return [sh[:, None, :RW_W], sh[:, None, RW_W:2 * RW_W], sh[:, None, 2 * RW_W:3 * RW_W], sh[:, None, 3 * RW_W:]]


def _mixers_recurrent(proj, nb, t_len, pk, seg, gla_s0, rw_s0, rw_shift0, t_valid):
    oa, gla_s = _gla(proj, nb, t_len, pk["gla_wg2"], pk["gla_bg"], pk["gla_ng"],
                     _blockdiag_in(jnp.swapaxes(gla_s0, 2, 3), GLA_HEADS // 2), t_valid)
    oc, rw_s = _rwkv(proj, nb, t_len, _split_shift(rw_shift0), pk["rw"], seg,
                     _blockdiag_in(rw_s0, RW_HEADS // 2), t_valid)
    gla_s = jnp.swapaxes(_blockdiag_out(gla_s, GLA_DV, GLA_DK), 2, 3)
    rw_s = _blockdiag_out(rw_s, RW_N, RW_N)
    return oa, oc, gla_s, rw_s


def kernel(x_prompt, x_sample, mem_prompt, cache_sb_k, cache_sb_v, page_table, state_gla, state_rwkv,
           state_rwkv_shift, cache_mem_k, cache_mem_v, ln_g, ln_b, ffn_w1, ffn_w3, ffn_w2, w_in, b_gate,
           gla_wg2, gla_bg, gla_norm_g, sb_bias, rw_mu, rw_w0, rw_w2, rw_a0, rw_a2, rw_g2, rw_kk, rw_ka,
           rw_rk, rw_lnx_g, rw_lnx_b, w_br, w_o, mem_wq, mem_wk, mem_wv, mem_wo):
    bp, tp, d = x_prompt.shape
    db, ts, _ = x_sample.shape
    mlen = mem_prompt.shape[1]
    n_pool = cache_sb_k.shape[1]
    assert tp % CHUNK == 0 and ts <= CHUNK

    lane = jnp.arange(2 * LANES)
    seg = (lane[:, None] // RW_N == lane[None, :] // RW_N).astype(BF16)
    seg = jnp.concatenate([seg, seg], axis=0)
    tq_sb = min(SB_TQ, tp)
    sidx = jnp.arange(min(SB_SUB, tq_sb))
    ms = -(sidx[:, None] >= sidx[None, :]).astype(BF16)
    kidx = jnp.arange(PAGE_SIZE)
    ms_s = -(kidx[:, None] > kidx[None, :]).astype(BF16)
    head_lane = (jnp.arange(SB_W)[None, :] // SB_DH == jnp.arange(SB_HEADS)[:, None]).astype(F32)
    cache_kt = cache_sb_k.transpose(0, 1, 3, 4, 2)
    cache_vt = cache_sb_v.transpose(0, 1, 3, 4, 2)
    tpad = 1 << (ts - 1).bit_length()

    yp = x_prompt.reshape(bp * tp, d)
    ys = x_sample.reshape(db * ts, d)
    memp = mem_prompt.reshape(bp * mlen, d)
    outs = {k: [] for k in ("kvp", "gp", "rp", "shp", "mk", "mv", "ks", "vs", "gs", "rs", "shs")}
    for l in range(DEPTH):
        pk = _pack_layer(l, ln_g, ln_b, ffn_w1, ffn_w3, ffn_w2, w_in, b_gate, gla_wg2, gla_bg, gla_norm_g, sb_bias,
                         rw_mu, rw_w0, rw_w2, rw_a0, rw_a2, rw_g2, rw_kk, rw_ka, rw_rk, rw_lnx_g, rw_lnx_b, w_br,
                         w_o, mem_wq, mem_wk, mem_wv, mem_wo)
        mkv = _matmul(memp, pk["mem_wkv"])
        mk, mv = mkv[:, :d].reshape(bp, mlen, d), mkv[:, d:].reshape(bp, mlen, d)
        x1 = _ffn_ln(yp, *pk["ffn"][0], pk["ln_g"][0], pk["ln_b"][0])
        proj, kv_new = _matmul(x1, pk["w_in"], wt=pk["w_kv_t"], rows_per_batch=tp)
        oa, oc, gla_s, rw_s = _mixers_recurrent(
            proj, bp, tp, pk, seg, jnp.zeros((bp, GLA_HEADS, GLA_DK, GLA_DV), F32),
            jnp.zeros((bp, RW_HEADS, RW_N, RW_N), F32), jnp.zeros((bp, RW_COLS), F32), None)
        bias_rows = jnp.broadcast_to(pk["sb_bias"].reshape(SB_HEADS // 2, 2, 1), (SB_HEADS // 2, 2, tq_sb))
        ob = _sb_prompt(proj, bp, tp, bias_rows, ms)
        x2 = _merge_ln(x1, oa, ob, oc, proj, pk["b_gate"], pk["w_br"], pk["w_o"], pk["ln_g"][1], pk["ln_b"][1])
        x3 = _memattn_ln(x2, pk["mem_wq"], pk["mem_wo"], mk, mv, pk["ln_g"][2], pk["ln_b"][2], tp)
        yp = _ffn_ln(x3, *pk["ffn"][1], pk["ln_g"][3], pk["ln_b"][3])
        proj3 = proj.reshape(bp, tp, P_COLS)
        outs["kvp"].append(kv_new)
        outs["gp"].append(gla_s)
        outs["rp"].append(rw_s)
        outs["shp"].append(proj3[:, -1, COL_RW:COL_RW + RW_COLS])
        outs["mk"].append(mk.reshape(bp, mlen, MEM_HEADS, d // MEM_HEADS))
        outs["mv"].append(mv.reshape(bp, mlen, MEM_HEADS, d // MEM_HEADS))
        x1 = _ffn_ln(ys, *pk["ffn"][0], pk["ln_g"][0], pk["ln_b"][0])
        proj = _matmul(x1, pk["w_in"])
        proj3 = proj.reshape(db, ts, P_COLS)
        proj_pad = jnp.pad(proj3, ((0, 0), (0, CHUNK - ts), (0, 0))).reshape(db * CHUNK, P_COLS)
        oa, oc, gla_s, rw_s = _mixers_recurrent(proj_pad, db, CHUNK, pk, seg, state_gla[l], state_rwkv[l],
                                                state_rwkv_shift[l], ts)
        oa = oa.reshape(db, CHUNK, GLA_V)[:, :ts].reshape(db * ts, GLA_V)
        oc = oc.reshape(db, CHUNK, RW_W)[:, :ts].reshape(db * ts, RW_W)
        qs = proj3[:, :, COL_SB_Q:COL_SB_Q + SB_W]
        ks_new = proj3[:, :, COL_SB_K:COL_SB_K + SB_W]
        vs_new = proj3[:, :, COL_SB_V:COL_SB_V + SB_W]
        qpad = jnp.pad(qs, ((0, 0), (0, tpad - ts), (0, 0)))
        qbd = (qpad[:, None, :, :] * head_lane[None, :, None, :]).reshape(db, SB_HEADS * tpad, SB_W)
        pad_page = ((0, 0), (0, 0), (0, PAGE_SIZE - ts))
        bias_col = jnp.repeat(pk["sb_bias"], tpad).reshape(SB_HEADS * tpad, 1)
        ob = _sb_sample(qbd, bias_col, jnp.pad(jnp.swapaxes(ks_new, 1, 2), pad_page),
                        jnp.pad(jnp.swapaxes(vs_new, 1, 2), pad_page), cache_kt, cache_vt, page_table, l, ms_s)
        ob = jnp.einsum("bhtgd,hg->btgd", ob.reshape(db, SB_HEADS, tpad, SB_HEADS, SB_DH)[:, :, :ts],
                        jnp.eye(SB_HEADS, dtype=F32)).reshape(db * ts, SB_W)
        x2 = _merge_ln(x1, oa, ob, oc, proj, pk["b_gate"], pk["w_br"], pk["w_o"], pk["ln_g"][1], pk["ln_b"][1])
        x3 = _memattn_ln(x2, pk["mem_wq"], pk["mem_wo"], cache_mem_k[l].reshape(db, mlen, d),
                         cache_mem_v[l].reshape(db, mlen, d), pk["ln_g"][2], pk["ln_b"][2], ts)
        ys = _ffn_ln(x3, *pk["ffn"][1], pk["ln_g"][3], pk["ln_b"][3])
        outs["ks"].append(ks_new.reshape(db, ts, SB_HEADS, SB_DH))
        outs["vs"].append(vs_new.reshape(db, ts, SB_HEADS, SB_DH))
        outs["gs"].append(gla_s)
        outs["rs"].append(rw_s)
        outs["shs"].append(proj3[:, -1, COL_RW:COL_RW + RW_COLS])
    st = lambda k: jnp.stack(outs[k])
    kvp = st("kvp").reshape(DEPTH, bp, 2, SB_HEADS, SB_DH, tp)
    heads_p = lambda a: a.transpose(0, 1, 4, 2, 3)
    return (yp.reshape(bp, tp, d), ys.reshape(db, ts, d), heads_p(kvp[:, :, 0]), heads_p(kvp[:, :, 1]),
            st("gp"), st("rp"), st("shp"),
            st("mk"), st("mv"), st("ks"), st("vs"), st("gs"), st("rs"), st("shs"))
```
